```python
import jax, jax.numpy as jnp
from jax import lax
import numpy as np

D_MODEL = 2048
BATCH = 2
SEQ = 8192
DEPTH = 1

CHUNK = 64
MEM_LEN = 256
D_MIX = D_MODEL
D_A = D_MIX // 2
D_B = D_MIX - D_A
A_HEADS = 8
A_HEAD_DIM = D_A // A_HEADS
SGU_BLOCK = 128
POOL_WINDOWS = (2, 4, 8, 16)
B_GROUPS = len(POOL_WINDOWS)
B_GROUP_DIM = D_B // B_GROUPS
D_IN = 2 * D_A + D_B
XATTN_HEADS = 4
XATTN_HEAD_DIM = D_MODEL // XATTN_HEADS
PEER_HEADS = 8
PEER_N_KEYS = 128
PEER_N_EXPERTS = PEER_N_KEYS * PEER_N_KEYS
PEER_TOPK = 16
PEER_D_KEY = 256
PEER_HALF = PEER_D_KEY // 2
PEER_TOKEN_BLOCK = 128
EPS = 1e-6

kernel_name = "hybrid_sgu_pool_peer_block"


def rmsnorm(x, g):
    xf = x.astype(jnp.float32)
    xf = xf * lax.rsqrt(jnp.mean(xf * xf, axis=-1, keepdims=True) + EPS)
    return xf.astype(x.dtype) * g


def layernorm(x, g, b):
    xf = x.astype(jnp.float32)
    mu = jnp.mean(xf, axis=-1, keepdims=True)
    xc = xf - mu
    xf = xc * lax.rsqrt(jnp.mean(xc * xc, axis=-1, keepdims=True) + EPS)
    return xf.astype(x.dtype) * g + b


def spatial_gating_unit(u, v, ln_g, ln_b, w_s, b_s):
    Bn, S, _ = u.shape
    nblk = S // SGU_BLOCK
    v = layernorm(v, ln_g, ln_b).reshape(Bn, nblk, SGU_BLOCK, A_HEADS, A_HEAD_DIM)
    pos = jnp.arange(SGU_BLOCK)
    mask = (pos[None, :] // CHUNK) <= (pos[:, None] // CHUNK)
    w = jnp.where(mask[None], w_s, jnp.zeros_like(w_s))
    mixed = jnp.einsum('hij,bnjhd->bnihd', w, v) + b_s.T[None, None, :, :, None]
    return u * mixed.reshape(Bn, S, D_A)


def multiscale_pool(xb, w_pool, pool_scale):
    Bn, S, _ = xb.shape
    xg = xb.reshape(Bn, S, B_GROUPS, B_GROUP_DIM)
    xf = xg.astype(jnp.float32)
    cs = jnp.cumsum(xf, axis=1)
    cs = jnp.concatenate([jnp.zeros_like(cs[:, :1]), cs], axis=1)
    t = jnp.arange(S)
    pooled = []
    for g, w in enumerate(POOL_WINDOWS):
        cs_g = cs[:, :, g]
        start = jnp.maximum(t + 1 - w, 0)
        window_sum = cs_g[:, 1:] - cs_g[:, start]
        count = (t + 1 - start).astype(jnp.float32)
        pooled.append(window_sum / count[None, :, None])
    pooled = jnp.stack(pooled, axis=2)
    y = (pooled - xf).astype(xb.dtype)
    y = jnp.einsum('bsgc,gcd->bsgd', y, w_pool)
    return y.reshape(Bn, S, D_B) * pool_scale


def memory_cross_attention(h, mem, g_q, g_mem, w_q, w_kv, w_o):
    Bn, S, D = h.shape
    n = rmsnorm(h, g_q)
    m = rmsnorm(mem, g_mem)
    q = (n @ w_q).reshape(Bn, S, XATTN_HEADS, XATTN_HEAD_DIM)
    kv = m @ w_kv
    k = kv[..., :D_MODEL].reshape(Bn, -1, XATTN_HEADS, XATTN_HEAD_DIM)
    v = kv[..., D_MODEL:].reshape(Bn, -1, XATTN_HEADS, XATTN_HEAD_DIM)
    scores = jnp.einsum('bshd,bmhd->bhsm', q, k).astype(jnp.float32) * (XATTN_HEAD_DIM ** -0.5)
    p = jax.nn.softmax(scores, axis=-1).astype(v.dtype)
    o = jnp.einsum('bhsm,bmhd->bshd', p, v).reshape(Bn, S, D)
    return o @ w_o


def peer_retrieve(n, w_query, sub_keys):
    T = n.shape[0]
    q = (n @ w_query).reshape(T, PEER_HEADS, 2, PEER_HALF)
    s = jnp.einsum('thpc,hpkc->thpk', q, sub_keys).astype(jnp.float32)
    top_s, top_i = lax.top_k(s, PEER_TOPK)
    cand_s = top_s[:, :, 0, :, None] + top_s[:, :, 1, None, :]
    cand_i = top_i[:, :, 0, :, None] * PEER_N_KEYS + top_i[:, :, 1, None, :]
    cand_s = cand_s.reshape(T, PEER_HEADS, PEER_TOPK * PEER_TOPK)
    cand_i = cand_i.reshape(T, PEER_HEADS, PEER_TOPK * PEER_TOPK)
    best_s, pos = lax.top_k(cand_s, PEER_TOPK)
    idx = jnp.take_along_axis(cand_i, pos, axis=-1)
    gates = jax.nn.softmax(best_s, axis=-1)
    return idx, gates


def peer_experts(n, idx, gates, u_table, v_table):
    T, D = n.shape
    nb = T // PEER_TOKEN_BLOCK

    def block(args):
        xb, ib, gb = args
        ue = jnp.take(u_table, ib, axis=0)
        a = jax.nn.gelu(jnp.einsum('td,thkd->thk', xb, ue), approximate=False)
        a = a * gb.astype(xb.dtype)
        ve = jnp.take(v_table, ib, axis=0)
        return jnp.einsum('thk,thkd->td', a, ve)

    out = lax.map(block, (n.reshape(nb, PEER_TOKEN_BLOCK, D),
                          idx.reshape(nb, PEER_TOKEN_BLOCK, PEER_HEADS, PEER_TOPK),
                          gates.reshape(nb, PEER_TOKEN_BLOCK, PEER_HEADS, PEER_TOPK)))
    return out.reshape(T, D)


def setup_inputs(seed: int = 0) -> dict:
    key = jax.random.key(seed)
    ks = jax.random.split(key, 24)
    f32 = jnp.float32
    D = D_MODEL

    def nrm(k, shape, scale):
        return jax.random.normal(k, shape, f32) * scale

    def gain(k, shape):
        return 1.0 + 0.05 * jax.random.normal(k, shape, f32)

    return {
        "x": nrm(ks[0], (BATCH, SEQ, D), 1.0),
        "mem": nrm(ks[1], (BATCH, MEM_LEN, D), 1.0),
        "g_mix": gain(ks[2], (DEPTH, D)),
        "w_in": nrm(ks[3], (DEPTH, D, D_IN), D ** -0.5),
        "sgu_ln_g": gain(ks[4], (DEPTH, D_A)),
        "sgu_ln_b": nrm(ks[5], (DEPTH, D_A), 0.02),
        "sgu_w": nrm(ks[6], (DEPTH, A_HEADS, SGU_BLOCK, SGU_BLOCK), 0.5 * SGU_BLOCK ** -0.5),
        "sgu_b": gain(ks[7], (DEPTH, A_HEADS, SGU_BLOCK)),
        "pool_w": nrm(ks[8], (DEPTH, B_GROUPS, B_GROUP_DIM, B_GROUP_DIM), B_GROUP_DIM ** -0.5),
        "pool_scale": 0.5 + 0.05 * jax.random.normal(ks[9], (DEPTH, D_B), f32),
        "w_out": nrm(ks[10], (DEPTH, D_MIX, D), D_MIX ** -0.5),
        "g_xattn": gain(ks[11], (DEPTH, D)),
        "g_mem": gain(ks[12], (DEPTH, D)),
        "xattn_w_q": nrm(ks[13], (DEPTH, D, D), D ** -0.5),
        "xattn_w_kv": nrm(ks[14], (DEPTH, D, 2 * D), D ** -0.5),
        "xattn_w_o": nrm(ks[15], (DEPTH, D, D), D ** -0.5),
        "g_peer": gain(ks[16], (DEPTH, D)),
        "peer_w_query": nrm(ks[17], (DEPTH, D, PEER_HEADS * PEER_D_KEY), D ** -0.5),
        "peer_sub_keys": nrm(ks[18], (DEPTH, PEER_HEADS, 2, PEER_N_KEYS, PEER_HALF), PEER_HALF ** -0.5),
        "peer_u": nrm(ks[19], (DEPTH, PEER_N_EXPERTS, D), D ** -0.5),
        "peer_v": nrm(ks[20], (DEPTH, PEER_N_EXPERTS, D), 0.7),
        "g_final": gain(ks[21], (D,)),
    }


def reference(x, mem, g_mix, w_in, sgu_ln_g, sgu_ln_b, sgu_w, sgu_b, pool_w, pool_scale, w_out,
              g_xattn, g_mem, xattn_w_q, xattn_w_kv, xattn_w_o,
              g_peer, peer_w_query, peer_sub_keys, peer_u, peer_v, g_final):
    Bn, S, D = x.shape
    h = x
    for l in range(DEPTH):
        n = rmsnorm(h, g_mix[l])
        proj = n @ w_in[l]
        u = jax.nn.gelu(proj[..., :D_A], approximate=False)
        v = jax.nn.gelu(proj[..., D_A:2 * D_A], approximate=False)
        xp = proj[..., 2 * D_A:]
        ya = spatial_gating_unit(u, v, sgu_ln_g[l], sgu_ln_b[l], sgu_w[l], sgu_b[l])
        yb = multiscale_pool(xp, pool_w[l], pool_scale[l])
        h = h + jnp.concatenate([ya, yb], axis=-1) @ w_out[l]
        h = h + memory_cross_attention(h, mem, g_xattn[l], g_mem[l],
                                       xattn_w_q[l], xattn_w_kv[l], xattn_w_o[l])
        n = rmsnorm(h, g_peer[l]).reshape(Bn * S, D)
        idx, gates = peer_retrieve(n, peer_w_query[l], peer_sub_keys[l])
        h = h + peer_experts(n, idx, gates, peer_u[l], peer_v[l]).reshape(Bn, S, D)
    return rmsnorm(h, g_final)
```

```python
import functools

import jax
import jax.numpy as jnp
from jax import lax
from jax.experimental import pallas as pl
from jax.experimental.pallas import tpu as pltpu

F32 = jnp.float32
BF16 = jnp.bfloat16
I32 = jnp.int32

EPS = 1e-6
CHUNK = 64
SGU_BLOCK = 128
A_HEADS = 8
POOL_WINDOWS = (2, 4, 8, 16)
POOL_HALO = 16
XATTN_HEADS = 4
PEER_HEADS = 8
PEER_TOPK = 16
LANES = 128
SUBLANES = 8
VMEM_LIMIT = 56 * 1024 * 1024

NT_DIMS = (((1,), (1,)), ((), ()))


def _rms(x, g):
    return x * lax.rsqrt(jnp.mean(x * x, axis=-1, keepdims=True) + EPS) * g


def _gelu(x):
    return 0.5 * x * (1.0 + lax.erf(x * 0.7071067811865476))


def _const_spec(shape):
    nd = len(shape)
    return pl.BlockSpec(shape, lambda i: (0,) * nd)


def _mixer_kernel(x_ref, g_ref, win_ref, lng_ref, lnb_ref, wsgu_ref, bsgu_ref,
                  wpool_ref, pscale_ref, wout_ref, o_ref, xpe_ref, cat_ref,
                  *, tm, blocks_per_seq, d_a, d_b):
    i = pl.program_id(0)
    seq_block = i % blocks_per_seq
    x = x_ref[...]
    nb = _rms(x, g_ref[...]).astype(BF16)

    v = _gelu(jnp.dot(nb, win_ref[:, d_a:2 * d_a], preferred_element_type=F32))
    mu = jnp.mean(v, axis=-1, keepdims=True)
    vc = v - mu
    v = vc * lax.rsqrt(jnp.mean(vc * vc, axis=-1, keepdims=True) + EPS)
    v = (v * lng_ref[...] + lnb_ref[...]).astype(BF16)
    u = _gelu(jnp.dot(nb, win_ref[:, 0:d_a], preferred_element_type=F32))
    hd = d_a // A_HEADS
    pi = lax.broadcasted_iota(I32, (SGU_BLOCK, SGU_BLOCK), 0) // CHUNK
    pj = lax.broadcasted_iota(I32, (SGU_BLOCK, SGU_BLOCK), 1) // CHUNK
    causal = pj <= pi
    for h in range(A_HEADS):
        w = jnp.where(causal, wsgu_ref[h], 0.0).astype(BF16)
        for r in range(tm // SGU_BLOCK):
            rows = slice(r * SGU_BLOCK, (r + 1) * SGU_BLOCK)
            cols = slice(h * hd, (h + 1) * hd)
            mixed = jnp.dot(w, v[rows, cols], preferred_element_type=F32) + bsgu_ref[h]
            cat_ref[rows, cols] = (u[rows, cols] * mixed).astype(BF16)

    xp = jnp.dot(nb, win_ref[:, 2 * d_a:], preferred_element_type=F32)
    tail = xpe_ref[tm:tm + POOL_HALO, :]
    xpe_ref[0:POOL_HALO, :] = jnp.where(seq_block == 0, 0.0, tail)
    xpe_ref[POOL_HALO:, :] = xp
    dg = d_b // len(POOL_WINDOWS)
    t_seq = seq_block * tm + lax.broadcasted_iota(I32, (tm, 1), 0)
    for g, w in enumerate(POOL_WINDOWS):
        cols = slice(g * dg, (g + 1) * dg)
        acc = xp[:, cols]
        for k in range(1, w):
            acc = acc + xpe_ref[POOL_HALO - k:POOL_HALO - k + tm, cols]
        count = jnp.minimum(t_seq + 1, w).astype(F32)
        y = (acc / count - xp[:, cols]).astype(BF16)
        yb = jnp.dot(y, wpool_ref[g], preferred_element_type=F32) * pscale_ref[:, cols]
        cat_ref[:, d_a + g * dg:d_a + (g + 1) * dg] = yb.astype(BF16)

    o_ref[...] = x + jnp.dot(cat_ref[...], wout_ref[...], preferred_element_type=F32)


def _mixer(x2, g_mix, w_in, ln_g, ln_b, sgu_w, sgu_b, pool_w, pool_scale, w_out, *, seq, tm):
    t, d = x2.shape
    d_a = ln_g.shape[-1]
    d_b = pool_scale.shape[-1]
    d_in = w_in.shape[-1]
    bsgu = jnp.broadcast_to(sgu_b[:, :, None], sgu_b.shape + (d_a // A_HEADS,))
    kern = functools.partial(_mixer_kernel, tm=tm, blocks_per_seq=seq // tm, d_a=d_a, d_b=d_b)
    return pl.pallas_call(
        kern,
        grid=(t // tm,),
        in_specs=[
            pl.BlockSpec((tm, d), lambda i: (i, 0)),
            _const_spec((1, d)),
            _const_spec((d, d_in)),
            _const_spec((1, d_a)),
            _const_spec((1, d_a)),
            _const_spec(sgu_w.shape),
            _const_spec(bsgu.shape),
            _const_spec(pool_w.shape),
            _const_spec((1, d_b)),
            _const_spec(w_out.shape),
        ],
        out_specs=pl.BlockSpec((tm, d), lambda i: (i, 0)),
        out_shape=jax.ShapeDtypeStruct((t, d), F32),
        scratch_shapes=[
            pltpu.VMEM((tm + POOL_HALO, d_b), F32),
            pltpu.VMEM((tm, d_a + d_b), BF16),
        ],
        compiler_params=pltpu.CompilerParams(
            dimension_semantics=("arbitrary",), vmem_limit_bytes=VMEM_LIMIT),
        name="mixer",
    )(x2, g_mix.reshape(1, d), w_in.astype(BF16), ln_g.reshape(1, d_a), ln_b.reshape(1, d_a),
      sgu_w, bsgu, pool_w.astype(BF16), pool_scale.reshape(1, d_b), w_out.astype(BF16))


def _kv_kernel(m_ref, g_ref, w_ref, o_ref):
    mb = _rms(m_ref[...], g_ref[...]).astype(BF16)
    o_ref[...] = jnp.dot(mb, w_ref[...], preferred_element_type=F32).astype(BF16)


def _kv(mem2, g_mem, w_kv, *, tn):
    m, d = mem2.shape
    n = w_kv.shape[-1]
    return pl.pallas_call(
        _kv_kernel,
        grid=(n // tn,),
        in_specs=[
            _const_spec((m, d)),
            _const_spec((1, d)),
            pl.BlockSpec((d, tn), lambda j: (0, j)),
        ],
        out_specs=pl.BlockSpec((m, tn), lambda j: (0, j)),
        out_shape=jax.ShapeDtypeStruct((m, n), BF16),
        compiler_params=pltpu.CompilerParams(
            dimension_semantics=("arbitrary",), vmem_limit_bytes=VMEM_LIMIT),
        name="memkv",
    )(mem2, g_mem.reshape(1, d), w_kv.astype(BF16))


def _xattn_kernel(h_ref, g_ref, wq_ref, k_ref, v_ref, wo_ref, o_ref, cat_ref, *, scale):
    h = h_ref[...]
    nb = _rms(h, g_ref[...]).astype(BF16)
    d = h.shape[-1]
    hd = d // XATTN_HEADS
    for a in range(XATTN_HEADS):
        cols = slice(a * hd, (a + 1) * hd)
        q = jnp.dot(nb, wq_ref[:, cols], preferred_element_type=F32).astype(BF16)
        s = lax.dot_general(q, k_ref[0, :, cols], NT_DIMS, preferred_element_type=F32) * scale
        s = s - jnp.max(s, axis=-1, keepdims=True)
        e = jnp.exp(s)
        p = (e / jnp.sum(e, axis=-1, keepdims=True)).astype(BF16)
        cat_ref[:, cols] = jnp.dot(p, v_ref[0, :, cols], preferred_element_type=F32).astype(BF16)
    o_ref[...] = h + jnp.dot(cat_ref[...], wo_ref[...], preferred_element_type=F32)


def _xattn(h2, g_x, w_q, k, v, w_o, *, seq, tm):
    t, d = h2.shape
    mlen = k.shape[1]
    bps = seq // tm
    kern = functools.partial(_xattn_kernel, scale=float(d // XATTN_HEADS) ** -0.5)
    return pl.pallas_call(
        kern,
        grid=(t // tm,),
        in_specs=[
            pl.BlockSpec((tm, d), lambda i: (i, 0)),
            _const_spec((1, d)),
            _const_spec((d, d)),
            pl.BlockSpec((1, mlen, d), lambda i: (i // bps, 0, 0)),
            pl.BlockSpec((1, mlen, d), lambda i: (i // bps, 0, 0)),
            _const_spec((d, d)),
        ],
        out_specs=pl.BlockSpec((tm, d), lambda i: (i, 0)),
        out_shape=jax.ShapeDtypeStruct((t, d), F32),
        scratch_shapes=[pltpu.VMEM((tm, d), BF16)],
        compiler_params=pltpu.CompilerParams(
            dimension_semantics=("arbitrary",), vmem_limit_bytes=VMEM_LIMIT),
        name="xattn",
    )(h2, g_x.reshape(1, d), w_q.astype(BF16), k, v, w_o.astype(BF16))


def _topk_rows(s, k):
    n = s.shape[0]
    row = lax.broadcasted_iota(I32, s.shape, 0)
    vals, ids = [], []
    for _ in range(k):
        m = jnp.max(s, axis=0, keepdims=True)
        ix = jnp.min(jnp.where(s == m, row, n), axis=0, keepdims=True)
        vals.append(m)
        ids.append(ix)
        s = jnp.where(row == ix, -jnp.inf, s)
    return jnp.concatenate(vals, axis=0), jnp.concatenate(ids, axis=0)


def _retrieve_kernel(h_ref, g_ref, wq_ref, keys_ref, idx_ref, gate_ref, *, n_keys):
    nb = _rms(h_ref[...], g_ref[...]).astype(BF16)
    tm = nb.shape[0]
    half = keys_ref.shape[-1]
    kk = PEER_TOPK
    ids_out, gates_out = [], []
    for hd in range(PEER_HEADS):
        top_s, top_i = [], []
        for p in range(2):
            cols = slice((2 * hd + p) * half, (2 * hd + p + 1) * half)
            q = jnp.dot(nb, wq_ref[:, cols], preferred_element_type=F32).astype(BF16)
            s = lax.dot_general(keys_ref[hd, p], q, NT_DIMS, preferred_element_type=F32)
            ts, ti = _topk_rows(s, kk)
            top_s.append(ts)
            top_i.append(ti)
        cand_s = (top_s[0][:, None, :] + top_s[1][None, :, :]).reshape(kk * kk, tm)
        cand_i = (top_i[0][:, None, :] * n_keys + top_i[1][None, :, :]).reshape(kk * kk, tm)
        pos = lax.broadcasted_iota(I32, (kk * kk, tm), 0)
        best_s, best_i = [], []
        for _ in range(kk):
            m = jnp.max(cand_s, axis=0, keepdims=True)
            at = jnp.min(jnp.where(cand_s == m, pos, kk * kk), axis=0, keepdims=True)
            hit = pos == at
            best_s.append(m)
            best_i.append(jnp.max(jnp.where(hit, cand_i, -1), axis=0, keepdims=True))
            cand_s = jnp.where(hit, -jnp.inf, cand_s)
        bs = jnp.concatenate(best_s, axis=0)
        e = jnp.exp(bs - bs[0:1, :])
        gates_out.append(e / jnp.sum(e, axis=0, keepdims=True))
        ids_out.append(jnp.concatenate(best_i, axis=0))
    idx_ref[...] = jnp.concatenate(ids_out, axis=0).T
    gate_ref[...] = jnp.concatenate(gates_out, axis=0).T


def _retrieve(h2, g_peer, w_query, sub_keys, *, tm):
    t, d = h2.shape
    dq = w_query.shape[-1]
    n_keys = sub_keys.shape[2]
    width = PEER_HEADS * PEER_TOPK
    kern = functools.partial(_retrieve_kernel, n_keys=n_keys)
    return pl.pallas_call(
        kern,
        grid=(t // tm,),
        in_specs=[
            pl.BlockSpec((tm, d), lambda i: (i, 0)),
            _const_spec((1, d)),
            _const_spec((d, dq)),
            _const_spec(sub_keys.shape),
        ],
        out_specs=[
            pl.BlockSpec((tm, width), lambda i: (i, 0)),
            pl.BlockSpec((tm, width), lambda i: (i, 0)),
        ],
        out_shape=[
            jax.ShapeDtypeStruct((t, width), I32),
            jax.ShapeDtypeStruct((t, width), F32),
        ],
        compiler_params=pltpu.CompilerParams(
            dimension_semantics=("arbitrary",), vmem_limit_bytes=VMEM_LIMIT),
        name="retrieve",
    )(h2, g_peer.reshape(1, d), w_query.astype(BF16), sub_keys.astype(BF16))


RING = SUBLANES


def _pack_tables(peer_u, peer_v):
    e, d = peer_u.shape

    def pack(tbl):
        bits = lax.bitcast_convert_type(tbl.astype(BF16), jnp.uint16).astype(jnp.uint32)
        word = bits[:, :d // 2] | (bits[:, d // 2:] << 16)
        return lax.bitcast_convert_type(word, I32).reshape(e, d // 2 // LANES, LANES)

    both = jnp.concatenate([pack(peer_u), pack(peer_v)], axis=1)
    return both.reshape(e, both.shape[1], 1, LANES)


def _unpack_pair(w):
    lo = pltpu.bitcast(w << 16, F32)
    hi = pltpu.bitcast(w & jnp.int32(-65536), F32)
    return lo, hi


def _experts_kernel(h_ref, gpeer_ref, gate_ref, gfin_ref, idx_hbm, tab_hbm, o_ref,
                    idx_smem, rows_ref, xn_ref, idx_sems, row_sems,
                    *, tb, width, chunks, final_norm):
    i = pl.program_id(0)
    nblocks = pl.num_programs(0)
    tbw = tb * width
    cur = i % 2
    ngroups = tb // RING

    def idx_copy(block, buf):
        return pltpu.make_async_copy(
            idx_hbm.at[pl.ds(block * tbw, tbw)], idx_smem.at[pl.ds(buf * tbw, tbw)],
            idx_sems.at[buf])

    def row_copy(expert, slot, j):
        return pltpu.make_async_copy(
            tab_hbm.at[expert], rows_ref.at[slot, :, pl.ds(j, 1), :], row_sems.at[slot])

    def issue(offset, slot):
        for j in range(width):
            row_copy(idx_smem[offset + j], slot, j).start(priority=j % 2)

    def wait(slot):
        for j in range(width):
            row_copy(0, slot, j).wait()

    @pl.when(i == 0)
    def _():
        idx_copy(0, 0).start()
        idx_copy(0, 0).wait()
        for r in range(RING):
            issue(r * width, r)

    nxt_block = jnp.minimum(i + 1, nblocks - 1)
    idx_copy(nxt_block, 1 - cur).start()
    h = h_ref[...]
    xn_ref[...] = _rms(h, gpeer_ref[...])
    idx_copy(nxt_block, 1 - cur).wait()

    eye = (lax.broadcasted_iota(I32, (width, LANES), 0)
           == lax.broadcasted_iota(I32, (width, LANES), 1))
    ones_rows = jnp.ones((SUBLANES, LANES), F32)
    ones_sq = jnp.ones((LANES, LANES), F32)
    egroups = width // SUBLANES

    def token(r, x_rows, gate_rows, o_rows):
        slot = r
        wait(slot)
        xs = [x_rows[r:r + 1, c * LANES:(c + 1) * LANES] for c in range(2 * chunks)]
        parts = []
        for jg in range(egroups):
            acc = jnp.zeros((SUBLANES, LANES), F32)
            for s in range(chunks):
                lo, hi = _unpack_pair(rows_ref[slot, s, jg * SUBLANES:(jg + 1) * SUBLANES, :])
                acc = acc + lo * xs[s] + hi * xs[s + chunks]
            parts.append(acc)
        acc_all = jnp.concatenate(parts, axis=0)
        a_row = lax.dot_general(ones_rows, acc_all, NT_DIMS, precision=lax.Precision.HIGHEST,
                                preferred_element_type=F32)[0:1, :]
        c_row = _gelu(a_row) * gate_rows[r:r + 1, :]
        c_col = jnp.dot(jnp.where(eye, c_row, 0.0), ones_sq, precision=lax.Precision.HIGHEST,
                        preferred_element_type=F32)
        outs = [jnp.zeros((SUBLANES, LANES), F32) for _ in range(2 * chunks)]
        for jg in range(egroups):
            cb = c_col[jg * SUBLANES:(jg + 1) * SUBLANES, :]
            for s in range(chunks):
                lo, hi = _unpack_pair(
                    rows_ref[slot, chunks + s, jg * SUBLANES:(jg + 1) * SUBLANES, :])
                outs[s] = outs[s] + cb * lo
                outs[s + chunks] = outs[s + chunks] + cb * hi
        for c in range(2 * chunks):
            o_rows[r:r + 1, c * LANES:(c + 1) * LANES] = jnp.sum(outs[c], axis=0, keepdims=True)

    def group(g, carry):
        base = pl.multiple_of(g * RING, RING)
        x_rows = xn_ref.at[pl.ds(base, RING)]
        gate_rows = gate_ref.at[pl.ds(base, RING)]
        o_rows = o_ref.at[pl.ds(base, RING)]
        nxt = jnp.where(g + 1 < ngroups, cur * tbw + (g + 1) * (RING * width), (1 - cur) * tbw)
        for r in range(RING):
            token(r, x_rows, gate_rows, o_rows)
            issue(nxt + r * width, r)
        return carry

    lax.fori_loop(0, ngroups, group, 0)

    @pl.when(i == nblocks - 1)
    def _():
        for r in range(RING):
            wait(r)

    y = h + o_ref[...]
    o_ref[...] = _rms(y, gfin_ref[...]) if final_norm else y


def _experts(h2, g_peer, gates, idx, table, g_final, *, tb, final_norm):
    t, d = h2.shape
    width = gates.shape[-1]
    chunks = table.shape[1] // 2
    kern = functools.partial(_experts_kernel, tb=tb, width=width, chunks=chunks,
                             final_norm=final_norm)
    return pl.pallas_call(
        kern,
        grid=(t // tb,),
        in_specs=[
            pl.BlockSpec((tb, d), lambda i: (i, 0)),
            _const_spec((1, d)),
            pl.BlockSpec((tb, width), lambda i: (i, 0)),
            _const_spec((1, d)),
            pl.BlockSpec(memory_space=pl.ANY),
            pl.BlockSpec(memory_space=pl.ANY),
        ],
        out_specs=pl.BlockSpec((tb, d), lambda i: (i, 0)),
        out_shape=jax.ShapeDtypeStruct((t, d), F32),
        scratch_shapes=[
            pltpu.SMEM((2 * tb * width,), I32),
            pltpu.VMEM((RING, 2 * chunks, width, LANES), I32),
            pltpu.VMEM((tb, d), F32),
            pltpu.SemaphoreType.DMA((2,)),
            pltpu.SemaphoreType.DMA((RING,)),
        ],
        compiler_params=pltpu.CompilerParams(
            dimension_semantics=("arbitrary",), vmem_limit_bytes=VMEM_LIMIT),
        name="experts",
    )(h2, g_peer.reshape(1, d), gates, g_final.reshape(1, d), idx.reshape(-1), table)


def kernel(x, mem, g_mix, w_in, sgu_ln_g, sgu_ln_b, sgu_w, sgu_b, pool_w, pool_scale, w_out,
           g_xattn, g_mem, xattn_w_q, xattn_w_kv, xattn_w_o,
           g_peer, peer_w_query, peer_sub_keys, peer_u, peer_v, g_final):
    bn, seq, d = x.shape
    depth = g_mix.shape[0]
    h = x.reshape(bn * seq, d)
    tm = min(256, seq)
    for l in range(depth):
        h = _mixer(h, g_mix[l], w_in[l], sgu_ln_g[l], sgu_ln_b[l], sgu_w[l], sgu_b[l],
                   pool_w[l], pool_scale[l], w_out[l], seq=seq, tm=tm)
        kv = _kv(mem.reshape(-1, d), g_mem[l], xattn_w_kv[l], tn=512)
        k = kv[:, :d].reshape(bn, -1, d)
        v = kv[:, d:].reshape(bn, -1, d)
        h = _xattn(h, g_xattn[l], xattn_w_q[l], k, v, xattn_w_o[l], seq=seq, tm=tm)
        idx, gates = _retrieve(h, g_peer[l], peer_w_query[l], peer_sub_keys[l], tm=tm)
        table = _pack_tables(peer_u[l], peer_v[l])
        h = _experts(h, g_peer[l], gates, idx, table, g_final, tb=min(128, seq),
                     final_norm=(l == depth - 1))
    return h.reshape(bn, seq, d)
```

```python
import functools

import jax
import jax.numpy as jnp
from jax import lax
from jax.experimental import pallas as pl
from jax.experimental.pallas import tpu as pltpu

F32 = jnp.float32
BF16 = jnp.bfloat16
I32 = jnp.int32

EPS = 1e-6
CHUNK = 64
SGU_BLOCK = 128
A_HEADS = 8
POOL_WINDOWS = (2, 4, 8, 16)
POOL_HALO = 16
XATTN_HEADS = 4
PEER_HEADS = 8
PEER_TOPK = 16
LANES = 128
SUBLANES = 8
VMEM_LIMIT = 56 * 1024 * 1024

NT_DIMS = (((1,), (1,)), ((), ()))


def _rms(x, g):
    return x * lax.rsqrt(jnp.mean(x * x, axis=-1, keepdims=True) + EPS) * g


def _gelu(x):
    return 0.5 * x * (1.0 + lax.erf(x * 0.7071067811865476))


def _const_spec(shape):
    nd = len(shape)
    return pl.BlockSpec(shape, lambda i: (0,) * nd)


def _mixer_kernel(x_ref, g_ref, win_ref, lng_ref, lnb_ref, wsgu_ref, bsgu_ref,
                  wpool_ref, pscale_ref, wout_ref, o_ref, xpe_ref, cat_ref,
                  *, tm, blocks_per_seq, d_a, d_b):
    i = pl.program_id(0)
    seq_block = i % blocks_per_seq
    x = x_ref[...]
    nb = _rms(x, g_ref[...]).astype(BF16)

    v = _gelu(jnp.dot(nb, win_ref[:, d_a:2 * d_a], preferred_element_type=F32))
    mu = jnp.mean(v, axis=-1, keepdims=True)
    vc = v - mu
    v = vc * lax.rsqrt(jnp.mean(vc * vc, axis=-1, keepdims=True) + EPS)
    v = (v * lng_ref[...] + lnb_ref[...]).astype(BF16)
    u = _gelu(jnp.dot(nb, win_ref[:, 0:d_a], preferred_element_type=F32))
    hd = d_a // A_HEADS
    pi = lax.broadcasted_iota(I32, (SGU_BLOCK, SGU_BLOCK), 0) // CHUNK
    pj = lax.broadcasted_iota(I32, (SGU_BLOCK, SGU_BLOCK), 1) // CHUNK
    causal = pj <= pi
    for h in range(A_HEADS):
        w = jnp.where(causal, wsgu_ref[h], 0.0).astype(BF16)
        for r in range(tm // SGU_BLOCK):
            rows = slice(r * SGU_BLOCK, (r + 1) * SGU_BLOCK)
            cols = slice(h * hd, (h + 1) * hd)
            mixed = jnp.dot(w, v[rows, cols], preferred_element_type=F32) + bsgu_ref[h]
            cat_ref[rows, cols] = (u[rows, cols] * mixed).astype(BF16)

    xp = jnp.dot(nb, win_ref[:, 2 * d_a:], preferred_element_type=F32)
    tail = xpe_ref[tm:tm + POOL_HALO, :]
    xpe_ref[0:POOL_HALO, :] = jnp.where(seq_block == 0, 0.0, tail)
    xpe_ref[POOL_HALO:, :] = xp
    dg = d_b // len(POOL_WINDOWS)
    t_seq = seq_block * tm + lax.broadcasted_iota(I32, (tm, 1), 0)
    for g, w in enumerate(POOL_WINDOWS):
        cols = slice(g * dg, (g + 1) * dg)
        acc = xp[:, cols]
        for k in range(1, w):
            acc = acc + xpe_ref[POOL_HALO - k:POOL_HALO - k + tm, cols]
        count = jnp.minimum(t_seq + 1, w).astype(F32)
        y = (acc / count - xp[:, cols]).astype(BF16)
        yb = jnp.dot(y, wpool_ref[g], preferred_element_type=F32) * pscale_ref[:, cols]
        cat_ref[:, d_a + g * dg:d_a + (g + 1) * dg] = yb.astype(BF16)

    o_ref[...] = x + jnp.dot(cat_ref[...], wout_ref[...], preferred_element_type=F32)


def _mixer(x2, g_mix, w_in, ln_g, ln_b, sgu_w, sgu_b, pool_w, pool_scale, w_out, *, seq, tm):
    t, d = x2.shape
    d_a = ln_g.shape[-1]
    d_b = pool_scale.shape[-1]
    d_in = w_in.shape[-1]
    bsgu = jnp.broadcast_to(sgu_b[:, :, None], sgu_b.shape + (d_a // A_HEADS,))
    kern = functools.partial(_mixer_kernel, tm=tm, blocks_per_seq=seq // tm, d_a=d_a, d_b=d_b)
    return pl.pallas_call(
        kern,
        grid=(t // tm,),
        in_specs=[
            pl.BlockSpec((tm, d), lambda i: (i, 0)),
            _const_spec((1, d)),
            _const_spec((d, d_in)),
            _const_spec((1, d_a)),
            _const_spec((1, d_a)),
            _const_spec(sgu_w.shape),
            _const_spec(bsgu.shape),
            _const_spec(pool_w.shape),
            _const_spec((1, d_b)),
            _const_spec(w_out.shape),
        ],
        out_specs=pl.BlockSpec((tm, d), lambda i: (i, 0)),
        out_shape=jax.ShapeDtypeStruct((t, d), F32),
        scratch_shapes=[
            pltpu.VMEM((tm + POOL_HALO, d_b), F32),
            pltpu.VMEM((tm, d_a + d_b), BF16),
        ],
        compiler_params=pltpu.CompilerParams(
            dimension_semantics=("arbitrary",), vmem_limit_bytes=VMEM_LIMIT),
        name="mixer",
    )(x2, g_mix.reshape(1, d), w_in.astype(BF16), ln_g.reshape(1, d_a), ln_b.reshape(1, d_a),
      sgu_w, bsgu, pool_w.astype(BF16), pool_scale.reshape(1, d_b), w_out.astype(BF16))


def _kv_kernel(m_ref, g_ref, w_ref, o_ref):
    mb = _rms(m_ref[...], g_ref[...]).astype(BF16)
    o_ref[...] = jnp.dot(mb, w_ref[...], preferred_element_type=F32).astype(BF16)


def _kv(mem2, g_mem, w_kv, *, tn):
    m, d = mem2.shape
    n = w_kv.shape[-1]
    return pl.pallas_call(
        _kv_kernel,
        grid=(n // tn,),
        in_specs=[
            _const_spec((m, d)),
            _const_spec((1, d)),
            pl.BlockSpec((d, tn), lambda j: (0, j)),
        ],
        out_specs=pl.BlockSpec((m, tn), lambda j: (0, j)),
        out_shape=jax.ShapeDtypeStruct((m, n), BF16),
        compiler_params=pltpu.CompilerParams(
            dimension_semantics=("arbitrary",), vmem_limit_bytes=VMEM_LIMIT),
        name="memkv",
    )(mem2, g_mem.reshape(1, d), w_kv.astype(BF16))


def _xattn_kernel(h_ref, g_ref, wq_ref, k_ref, v_ref, wo_ref, o_ref, cat_ref, *, scale):
    h = h_ref[...]
    nb = _rms(h, g_ref[...]).astype(BF16)
    d = h.shape[-1]
    hd = d // XATTN_HEADS
    for a in range(XATTN_HEADS):
        cols = slice(a * hd, (a + 1) * hd)
        q = jnp.dot(nb, wq_ref[:, cols], preferred_element_type=F32).astype(BF16)
        s = lax.dot_general(q, k_ref[0, :, cols], NT_DIMS, preferred_element_type=F32) * scale
        s = s - jnp.max(s, axis=-1, keepdims=True)
        e = jnp.exp(s)
        p = (e / jnp.sum(e, axis=-1, keepdims=True)).astype(BF16)
        cat_ref[:, cols] = jnp.dot(p, v_ref[0, :, cols], preferred_element_type=F32).astype(BF16)
    o_ref[...] = h + jnp.dot(cat_ref[...], wo_ref[...], preferred_element_type=F32)


def _xattn(h2, g_x, w_q, k, v, w_o, *, seq, tm):
    t, d = h2.shape
    mlen = k.shape[1]
    bps = seq // tm
    kern = functools.partial(_xattn_kernel, scale=float(d // XATTN_HEADS) ** -0.5)
    return pl.pallas_call(
        kern,
        grid=(t // tm,),
        in_specs=[
            pl.BlockSpec((tm, d), lambda i: (i, 0)),
            _const_spec((1, d)),
            _const_spec((d, d)),
            pl.BlockSpec((1, mlen, d), lambda i: (i // bps, 0, 0)),
            pl.BlockSpec((1, mlen, d), lambda i: (i // bps, 0, 0)),
            _const_spec((d, d)),
        ],
        out_specs=pl.BlockSpec((tm, d), lambda i: (i, 0)),
        out_shape=jax.ShapeDtypeStruct((t, d), F32),
        scratch_shapes=[pltpu.VMEM((tm, d), BF16)],
        compiler_params=pltpu.CompilerParams(
            dimension_semantics=("arbitrary",), vmem_limit_bytes=VMEM_LIMIT),
        name="xattn",
    )(h2, g_x.reshape(1, d), w_q.astype(BF16), k, v, w_o.astype(BF16))


def _topk_rows(s, k):
    n = s.shape[0]
    row = lax.broadcasted_iota(I32, s.shape, 0)
    vals, ids = [], []
    for _ in range(k):
        m = jnp.max(s, axis=0, keepdims=True)
        ix = jnp.min(jnp.where(s == m, row, n), axis=0, keepdims=True)
        vals.append(m)
        ids.append(ix)
        s = jnp.where(row == ix, -jnp.inf, s)
    return jnp.concatenate(vals, axis=0), jnp.concatenate(ids, axis=0)


def _retrieve_kernel(h_ref, g_ref, wq_ref, keys_ref, idx_ref, gate_ref, *, n_keys):
    nb = _rms(h_ref[...], g_ref[...]).astype(BF16)
    tm = nb.shape[0]
    half = keys_ref.shape[-1]
    kk = PEER_TOPK
    ids_out, gates_out = [], []
    for hd in range(PEER_HEADS):
        top_s, top_i = [], []
        for p in range(2):
            cols = slice((2 * hd + p) * half, (2 * hd + p + 1) * half)
            q = jnp.dot(nb, wq_ref[:, cols], preferred_element_type=F32).astype(BF16)
            s = lax.dot_general(keys_ref[hd, p], q, NT_DIMS, preferred_element_type=F32)
            ts, ti = _topk_rows(s, kk)
            top_s.append(ts)
            top_i.append(ti)
        cand_s = (top_s[0][:, None, :] + top_s[1][None, :, :]).reshape(kk * kk, tm)
        cand_i = (top_i[0][:, None, :] * n_keys + top_i[1][None, :, :]).reshape(kk * kk, tm)
        pos = lax.broadcasted_iota(I32, (kk * kk, tm), 0)
        best_s, best_i = [], []
        for _ in range(kk):
            m = jnp.max(cand_s, axis=0, keepdims=True)
            at = jnp.min(jnp.where(cand_s == m, pos, kk * kk), axis=0, keepdims=True)
            hit = pos == at
            best_s.append(m)
            best_i.append(jnp.max(jnp.where(hit, cand_i, -1), axis=0, keepdims=True))
            cand_s = jnp.where(hit, -jnp.inf, cand_s)
        bs = jnp.concatenate(best_s, axis=0)
        e = jnp.exp(bs - bs[0:1, :])
        gates_out.append(e / jnp.sum(e, axis=0, keepdims=True))
        ids_out.append(jnp.concatenate(best_i, axis=0))
    idx_ref[...] = jnp.concatenate(ids_out, axis=0).T
    gate_ref[...] = jnp.concatenate(gates_out, axis=0).T


def _retrieve(h2, g_peer, w_query, sub_keys, *, tm):
    t, d = h2.shape
    dq = w_query.shape[-1]
    n_keys = sub_keys.shape[2]
    width = PEER_HEADS * PEER_TOPK
    kern = functools.partial(_retrieve_kernel, n_keys=n_keys)
    return pl.pallas_call(
        kern,
        grid=(t // tm,),
        in_specs=[
            pl.BlockSpec((tm, d), lambda i: (i, 0)),
            _const_spec((1, d)),
            _const_spec((d, dq)),
            _const_spec(sub_keys.shape),
        ],
        out_specs=[
            pl.BlockSpec((tm, width), lambda i: (i, 0)),
            pl.BlockSpec((tm, width), lambda i: (i, 0)),
        ],
        out_shape=[
            jax.ShapeDtypeStruct((t, width), I32),
            jax.ShapeDtypeStruct((t, width), F32),
        ],
        compiler_params=pltpu.CompilerParams(
            dimension_semantics=("arbitrary",), vmem_limit_bytes=VMEM_LIMIT),
        name="retrieve",
    )(h2, g_peer.reshape(1, d), w_query.astype(BF16), sub_keys.astype(BF16))


RING = SUBLANES


def _pack_tables(peer_u, peer_v):
    e, d = peer_u.shape

    def pack(tbl):
        bits = lax.bitcast_convert_type(tbl.astype(BF16), jnp.uint16).astype(jnp.uint32)
        word = bits[:, :d // 2] | (bits[:, d // 2:] << 16)
        return lax.bitcast_convert_type(word, I32).reshape(e, d // 2 // LANES, LANES)

    both = jnp.concatenate([pack(peer_u), pack(peer_v)], axis=1)
    return both.reshape(e, both.shape[1], 1, LANES)


def _unpack_pair(w):
    lo = pltpu.bitcast(w << 16, F32)
    hi = pltpu.bitcast(w & jnp.int32(-65536), F32)
    return lo, hi


def _experts_kernel(h_ref, gpeer_ref, gate_ref, gfin_ref, idx_hbm, tab_hbm, o_ref,
                    idx_smem, rows_a, rows_b, xn_ref, idx_sems, sems_a, sems_b,
                    *, tb, width, chunks, final_norm):
    i = pl.program_id(0)
    nblocks = pl.num_programs(0)
    tbw = tb * width
    cur = i % 2
    ngroups = tb // RING
    set_a = (rows_a, sems_a)
    set_b = (rows_b, sems_b)

    def idx_copy(block, buf):
        return pltpu.make_async_copy(
            idx_hbm.at[pl.ds(block * tbw, tbw)], idx_smem.at[pl.ds(buf * tbw, tbw)],
            idx_sems.at[buf])

    def row_copy(expert, rset, r, j):
        rows_ref, sems = rset
        return pltpu.make_async_copy(
            tab_hbm.at[expert], rows_ref.at[r, :, pl.ds(j, 1), :], sems.at[r])

    def issue(offset, rset):
        for r in range(RING):
            for j in range(width):
                row_copy(idx_smem[offset + r * width + j], rset, r, j).start(priority=j % 2)

    def wait(rset):
        for r in range(RING):
            for j in range(width):
                row_copy(0, rset, r, j).wait()

    @pl.when(i == 0)
    def _():
        idx_copy(0, 0).start()
        idx_copy(0, 0).wait()
        issue(0, set_a)

    nxt_block = jnp.minimum(i + 1, nblocks - 1)
    idx_copy(nxt_block, 1 - cur).start()
    h = h_ref[...]
    xn_ref[...] = _rms(h, gpeer_ref[...])
    idx_copy(nxt_block, 1 - cur).wait()

    eye = (lax.broadcasted_iota(I32, (width, LANES), 0)
           == lax.broadcasted_iota(I32, (width, LANES), 1)).astype(F32)
    lane = lax.broadcasted_iota(I32, (width, 2 * RING), 1)
    egroups = width // SUBLANES

    def group(g, this_set, next_set):
        rows_ref = this_set[0]
        base = pl.multiple_of(g * RING, RING)
        wait(this_set)
        nxt = jnp.where(g + 1 < ngroups, cur * tbw + (g + 1) * (RING * width), (1 - cur) * tbw)
        issue(nxt, next_set)

        x8 = xn_ref[pl.ds(base, RING), :]
        x_hi = x8.astype(BF16).astype(F32)
        x_lo = (x8 - x_hi).astype(BF16).astype(F32)
        x_cat = jnp.concatenate([x_hi, x_lo], axis=0)
        a_all = jnp.zeros((width, 2 * RING), F32)
        for r in range(RING):
            halves = [_unpack_pair(rows_ref[r, s]) for s in range(chunks)]
            lhs = jnp.concatenate([p[0] for p in halves] + [p[1] for p in halves], axis=1)
            a_r = lax.dot_general(lhs, x_cat, NT_DIMS, preferred_element_type=F32)
            a_all = jnp.where((lane == r) | (lane == RING + r), a_r, a_all)
        a_col = a_all[:, :RING] + a_all[:, RING:]
        g_col = lax.dot_general(eye, gate_ref[pl.ds(base, RING), :], NT_DIMS,
                                precision=lax.Precision.HIGHEST, preferred_element_type=F32)
        c_all = _gelu(a_col) * g_col

        o_rows = o_ref.at[pl.ds(base, RING)]
        for r in range(RING):
            c_col = jnp.broadcast_to(c_all[:, r:r + 1], (width, LANES))
            outs = [jnp.zeros((SUBLANES, LANES), F32) for _ in range(2 * chunks)]
            for jg in range(egroups):
                rows = slice(jg * SUBLANES, (jg + 1) * SUBLANES)
                cb = c_col[rows, :]
                for s in range(chunks):
                    lo, hi = _unpack_pair(rows_ref[r, chunks + s, rows, :])
                    outs[s] = outs[s] + cb * lo
                    outs[s + chunks] = outs[s + chunks] + cb * hi
            for c in range(2 * chunks):
                o_rows[r:r + 1, c * LANES:(c + 1) * LANES] = jnp.sum(outs[c], axis=0, keepdims=True)

    def group_pair(k, carry):
        group(2 * k, set_a, set_b)
        group(2 * k + 1, set_b, set_a)
        return carry

    lax.fori_loop(0, ngroups // 2, group_pair, 0)

    @pl.when(i == nblocks - 1)
    def _():
        wait(set_a)

    y = h + o_ref[...]
    o_ref[...] = _rms(y, gfin_ref[...]) if final_norm else y


def _experts(h2, g_peer, gates, idx, table, g_final, *, tb, final_norm):
    t, d = h2.shape
    width = gates.shape[-1]
    chunks = table.shape[1] // 2
    assert (tb // RING) % 2 == 0, "token groups alternate between two slot sets per block"
    kern = functools.partial(_experts_kernel, tb=tb, width=width, chunks=chunks,
                             final_norm=final_norm)
    return pl.pallas_call(
        kern,
        grid=(t // tb,),
        in_specs=[
            pl.BlockSpec((tb, d), lambda i: (i, 0)),
            _const_spec((1, d)),
            pl.BlockSpec((tb, width), lambda i: (i, 0)),
            _const_spec((1, d)),
            pl.BlockSpec(memory_space=pl.ANY),
            pl.BlockSpec(memory_space=pl.ANY),
        ],
        out_specs=pl.BlockSpec((tb, d), lambda i: (i, 0)),
        out_shape=jax.ShapeDtypeStruct((t, d), F32),
        scratch_shapes=[
            pltpu.SMEM((2 * tb * width,), I32),
            pltpu.VMEM((RING, 2 * chunks, width, LANES), I32),
            pltpu.VMEM((RING, 2 * chunks, width, LANES), I32),
            pltpu.VMEM((tb, d), F32),
            pltpu.SemaphoreType.DMA((2,)),
            pltpu.SemaphoreType.DMA((RING,)),
            pltpu.SemaphoreType.DMA((RING,)),
        ],
        compiler_params=pltpu.CompilerParams(
            dimension_semantics=("arbitrary",), vmem_limit_bytes=VMEM_LIMIT),
        name="experts",
    )(h2, g_peer.reshape(1, d), gates, g_final.reshape(1, d), idx.reshape(-1), table)


def kernel(x, mem, g_mix, w_in, sgu_ln_g, sgu_ln_b, sgu_w, sgu_b, pool_w, pool_scale, w_out,
           g_xattn, g_mem, xattn_w_q, xattn_w_kv, xattn_w_o,
           g_peer, peer_w_query, peer_sub_keys, peer_u, peer_v, g_final):
    bn, seq, d = x.shape
    depth = g_mix.shape[0]
    h = x.reshape(bn * seq, d)
    tm = min(256, seq)
    for l in range(depth):
        h = _mixer(h, g_mix[l], w_in[l], sgu_ln_g[l], sgu_ln_b[l], sgu_w[l], sgu_b[l],
                   pool_w[l], pool_scale[l], w_out[l], seq=seq, tm=tm)
        kv = _kv(mem.reshape(-1, d), g_mem[l], xattn_w_kv[l], tn=512)
        k = kv[:, :d].reshape(bn, -1, d)
        v = kv[:, d:].reshape(bn, -1, d)
        h = _xattn(h, g_xattn[l], xattn_w_q[l], k, v, xattn_w_o[l], seq=seq, tm=tm)
        idx, gates = _retrieve(h, g_peer[l], peer_w_query[l], peer_sub_keys[l], tm=tm)
        table = _pack_tables(peer_u[l], peer_v[l])
        h = _experts(h, g_peer[l], gates, idx, table, g_final, tb=min(128, seq),
                     final_norm=(l == depth - 1))
    return h.reshape(bn, seq, d)
```

```python
import functools

import jax
import jax.numpy as jnp
from jax import lax
from jax.experimental import pallas as pl
from jax.experimental.pallas import tpu as pltpu

F32 = jnp.float32
BF16 = jnp.bfloat16
I32 = jnp.int32

EPS = 1e-6
CHUNK = 64
SGU_BLOCK = 128
A_HEADS = 8
POOL_WINDOWS = (2, 4, 8, 16)
POOL_HALO = 16
XATTN_HEADS = 4
PEER_HEADS = 8
PEER_TOPK = 16
LANES = 128
SUBLANES = 8
VMEM_LIMIT = 56 * 1024 * 1024

NT_DIMS = (((1,), (1,)), ((), ()))


def _rms(x, g):
    return x * lax.rsqrt(jnp.mean(x * x, axis=-1, keepdims=True) + EPS) * g


def _gelu(x):
    return 0.5 * x * (1.0 + lax.erf(x * 0.7071067811865476))


def _const_spec(shape):
    nd = len(shape)
    return pl.BlockSpec(shape, lambda i: (0,) * nd)


def _mixer_kernel(x_ref, g_ref, win_ref, lng_ref, lnb_ref, wsgu_ref, bsgu_ref,
                  wpool_ref, pscale_ref, wout_ref, o_ref, xpe_ref, cat_ref,
                  *, tm, blocks_per_seq, d_a, d_b):
    i = pl.program_id(0)
    seq_block = i % blocks_per_seq
    x = x_ref[...]
    nb = _rms(x, g_ref[...]).astype(BF16)

    v = _gelu(jnp.dot(nb, win_ref[:, d_a:2 * d_a], preferred_element_type=F32))
    mu = jnp.mean(v, axis=-1, keepdims=True)
    vc = v - mu
    v = vc * lax.rsqrt(jnp.mean(vc * vc, axis=-1, keepdims=True) + EPS)
    v = (v * lng_ref[...] + lnb_ref[...]).astype(BF16)
    u = _gelu(jnp.dot(nb, win_ref[:, 0:d_a], preferred_element_type=F32))
    hd = d_a // A_HEADS
    pi = lax.broadcasted_iota(I32, (SGU_BLOCK, SGU_BLOCK), 0) // CHUNK
    pj = lax.broadcasted_iota(I32, (SGU_BLOCK, SGU_BLOCK), 1) // CHUNK
    causal = pj <= pi
    for h in range(A_HEADS):
        w = jnp.where(causal, wsgu_ref[h], 0.0).astype(BF16)
        for r in range(tm // SGU_BLOCK):
            rows = slice(r * SGU_BLOCK, (r + 1) * SGU_BLOCK)
            cols = slice(h * hd, (h + 1) * hd)
            mixed = jnp.dot(w, v[rows, cols], preferred_element_type=F32) + bsgu_ref[h]
            cat_ref[rows, cols] = (u[rows, cols] * mixed).astype(BF16)

    xp = jnp.dot(nb, win_ref[:, 2 * d_a:], preferred_element_type=F32)
    tail = xpe_ref[tm:tm + POOL_HALO, :]
    xpe_ref[0:POOL_HALO, :] = jnp.where(seq_block == 0, 0.0, tail)
    xpe_ref[POOL_HALO:, :] = xp
    dg = d_b // len(POOL_WINDOWS)
    t_seq = seq_block * tm + lax.broadcasted_iota(I32, (tm, 1), 0)
    for g, w in enumerate(POOL_WINDOWS):
        cols = slice(g * dg, (g + 1) * dg)
        acc = xp[:, cols]
        for k in range(1, w):
            acc = acc + xpe_ref[POOL_HALO - k:POOL_HALO - k + tm, cols]
        count = jnp.minimum(t_seq + 1, w).astype(F32)
        y = (acc / count - xp[:, cols]).astype(BF16)
        yb = jnp.dot(y, wpool_ref[g], preferred_element_type=F32) * pscale_ref[:, cols]
        cat_ref[:, d_a + g * dg:d_a + (g + 1) * dg] = yb.astype(BF16)

    o_ref[...] = x + jnp.dot(cat_ref[...], wout_ref[...], preferred_element_type=F32)


def _mixer(x2, g_mix, w_in, ln_g, ln_b, sgu_w, sgu_b, pool_w, pool_scale, w_out, *, seq, tm):
    t, d = x2.shape
    d_a = ln_g.shape[-1]
    d_b = pool_scale.shape[-1]
    d_in = w_in.shape[-1]
    bsgu = jnp.broadcast_to(sgu_b[:, :, None], sgu_b.shape + (d_a // A_HEADS,))
    kern = functools.partial(_mixer_kernel, tm=tm, blocks_per_seq=seq // tm, d_a=d_a, d_b=d_b)
    return pl.pallas_call(
        kern,
        grid=(t // tm,),
        in_specs=[
            pl.BlockSpec((tm, d), lambda i: (i, 0)),
            _const_spec((1, d)),
            _const_spec((d, d_in)),
            _const_spec((1, d_a)),
            _const_spec((1, d_a)),
            _const_spec(sgu_w.shape),
            _const_spec(bsgu.shape),
            _const_spec(pool_w.shape),
            _const_spec((1, d_b)),
            _const_spec(w_out.shape),
        ],
        out_specs=pl.BlockSpec((tm, d), lambda i: (i, 0)),
        out_shape=jax.ShapeDtypeStruct((t, d), F32),
        scratch_shapes=[
            pltpu.VMEM((tm + POOL_HALO, d_b), F32),
            pltpu.VMEM((tm, d_a + d_b), BF16),
        ],
        compiler_params=pltpu.CompilerParams(
            dimension_semantics=("arbitrary",), vmem_limit_bytes=VMEM_LIMIT),
        name="mixer",
    )(x2, g_mix.reshape(1, d), w_in.astype(BF16), ln_g.reshape(1, d_a), ln_b.reshape(1, d_a),
      sgu_w, bsgu, pool_w.astype(BF16), pool_scale.reshape(1, d_b), w_out.astype(BF16))


def _kv_kernel(m_ref, g_ref, w_ref, o_ref):
    mb = _rms(m_ref[...], g_ref[...]).astype(BF16)
    o_ref[...] = jnp.dot(mb, w_ref[...], preferred_element_type=F32).astype(BF16)


def _kv(mem2, g_mem, w_kv, *, tn):
    m, d = mem2.shape
    n = w_kv.shape[-1]
    return pl.pallas_call(
        _kv_kernel,
        grid=(n // tn,),
        in_specs=[
            _const_spec((m, d)),
            _const_spec((1, d)),
            pl.BlockSpec((d, tn), lambda j: (0, j)),
        ],
        out_specs=pl.BlockSpec((m, tn), lambda j: (0, j)),
        out_shape=jax.ShapeDtypeStruct((m, n), BF16),
        compiler_params=pltpu.CompilerParams(
            dimension_semantics=("arbitrary",), vmem_limit_bytes=VMEM_LIMIT),
        name="memkv",
    )(mem2, g_mem.reshape(1, d), w_kv.astype(BF16))


def _xattn_kernel(h_ref, g_ref, wq_ref, k_ref, v_ref, wo_ref, o_ref, cat_ref, *, scale):
    h = h_ref[...]
    nb = _rms(h, g_ref[...]).astype(BF16)
    d = h.shape[-1]
    hd = d // XATTN_HEADS
    for a in range(XATTN_HEADS):
        cols = slice(a * hd, (a + 1) * hd)
        q = jnp.dot(nb, wq_ref[:, cols], preferred_element_type=F32).astype(BF16)
        s = lax.dot_general(q, k_ref[0, :, cols], NT_DIMS, preferred_element_type=F32) * scale
        s = s - jnp.max(s, axis=-1, keepdims=True)
        e = jnp.exp(s)
        p = (e / jnp.sum(e, axis=-1, keepdims=True)).astype(BF16)
        cat_ref[:, cols] = jnp.dot(p, v_ref[0, :, cols], preferred_element_type=F32).astype(BF16)
    o_ref[...] = h + jnp.dot(cat_ref[...], wo_ref[...], preferred_element_type=F32)


def _xattn(h2, g_x, w_q, k, v, w_o, *, seq, tm):
    t, d = h2.shape
    mlen = k.shape[1]
    bps = seq // tm
    kern = functools.partial(_xattn_kernel, scale=float(d // XATTN_HEADS) ** -0.5)
    return pl.pallas_call(
        kern,
        grid=(t // tm,),
        in_specs=[
            pl.BlockSpec((tm, d), lambda i: (i, 0)),
            _const_spec((1, d)),
            _const_spec((d, d)),
            pl.BlockSpec((1, mlen, d), lambda i: (i // bps, 0, 0)),
            pl.BlockSpec((1, mlen, d), lambda i: (i // bps, 0, 0)),
            _const_spec((d, d)),
        ],
        out_specs=pl.BlockSpec((tm, d), lambda i: (i, 0)),
        out_shape=jax.ShapeDtypeStruct((t, d), F32),
        scratch_shapes=[pltpu.VMEM((tm, d), BF16)],
        compiler_params=pltpu.CompilerParams(
            dimension_semantics=("arbitrary",), vmem_limit_bytes=VMEM_LIMIT),
        name="xattn",
    )(h2, g_x.reshape(1, d), w_q.astype(BF16), k, v, w_o.astype(BF16))


def _topk_rows(s, k):
    n = s.shape[0]
    row = lax.broadcasted_iota(I32, s.shape, 0)
    vals, ids = [], []
    for _ in range(k):
        m = jnp.max(s, axis=0, keepdims=True)
        ix = jnp.min(jnp.where(s == m, row, n), axis=0, keepdims=True)
        vals.append(m)
        ids.append(ix)
        s = jnp.where(row == ix, -jnp.inf, s)
    return jnp.concatenate(vals, axis=0), jnp.concatenate(ids, axis=0)


_STAIRS = (
    ((0, 0, 8, 0),),
    ((0, 8, 8, 0),),
    ((1, 0, 8, 0),),
    ((2, 0, 5, 0), (4, 0, 3, 5)),
    ((3, 0, 4, 0), (5, 0, 2, 4), (6, 0, 2, 6)),
    ((7, 0, 2, 0),) + tuple((a, 0, 1, a - 6) for a in range(8, 14)),
    ((14, 0, 1, 0), (15, 0, 1, 1)),
)


def _stair_candidates(top_s, top_i, n_keys):
    kk = PEER_TOPK
    lanes = top_s[0].shape[1]
    sub = lax.broadcasted_iota(I32, (SUBLANES, lanes), 0)
    vals, ids, poss = [], [], []
    for segs in _STAIRS:
        v = jnp.full((SUBLANES, lanes), -jnp.inf, F32)
        ix = jnp.zeros((SUBLANES, lanes), I32)
        ps = jnp.full((SUBLANES, lanes), kk * kk, I32)
        for a, b0, nb, row in segs:
            blk = slice(b0, b0 + SUBLANES)
            s2, i2 = top_s[1][blk, :], top_i[1][blk, :]
            if row:
                s2, i2 = pltpu.roll(s2, row, 0), pltpu.roll(i2, row, 0)
            here = (sub >= row) & (sub < row + nb)
            v = jnp.where(here, top_s[0][a:a + 1, :] + s2, v)
            ix = jnp.where(here, top_i[0][a:a + 1, :] * n_keys + i2, ix)
            ps = jnp.where(here, a * kk + b0 + sub - row, ps)
        vals.append(v)
        ids.append(ix)
        poss.append(ps)
    return (jnp.concatenate(vals, axis=0), jnp.concatenate(ids, axis=0),
            jnp.concatenate(poss, axis=0))


def _retrieve_kernel(h_ref, g_ref, wq_ref, keys_ref, idx_ref, gate_ref, *, n_keys):
    nb = _rms(h_ref[...], g_ref[...]).astype(BF16)
    tm = nb.shape[0]
    half = keys_ref.shape[-1]
    kk = PEER_TOPK
    ids_out, gates_out = [], []
    for hd in range(PEER_HEADS):
        top_s, top_i = [], []
        for p in range(2):
            cols = slice((2 * hd + p) * half, (2 * hd + p + 1) * half)
            q = jnp.dot(nb, wq_ref[:, cols], preferred_element_type=F32).astype(BF16)
            s = lax.dot_general(keys_ref[hd, p], q, NT_DIMS, preferred_element_type=F32)
            ts, ti = _topk_rows(s, kk)
            top_s.append(ts)
            top_i.append(ti)
        cand_s, cand_i, pos = _stair_candidates(top_s, top_i, n_keys)
        best_s, best_i = [], []
        for _ in range(kk):
            m = jnp.max(cand_s, axis=0, keepdims=True)
            at = jnp.min(jnp.where(cand_s == m, pos, kk * kk), axis=0, keepdims=True)
            hit = pos == at
            best_s.append(m)
            best_i.append(jnp.max(jnp.where(hit, cand_i, -1), axis=0, keepdims=True))
            cand_s = jnp.where(hit, -jnp.inf, cand_s)
        bs = jnp.concatenate(best_s, axis=0)
        e = jnp.exp(bs - bs[0:1, :])
        gates_out.append(e / jnp.sum(e, axis=0, keepdims=True))
        ids_out.append(jnp.concatenate(best_i, axis=0))
    idx_ref[...] = jnp.concatenate(ids_out, axis=0).T
    gate_ref[...] = jnp.concatenate(gates_out, axis=0).T


def _retrieve(h2, g_peer, w_query, sub_keys, *, tm):
    t, d = h2.shape
    dq = w_query.shape[-1]
    n_keys = sub_keys.shape[2]
    width = PEER_HEADS * PEER_TOPK
    kern = functools.partial(_retrieve_kernel, n_keys=n_keys)
    return pl.pallas_call(
        kern,
        grid=(t // tm,),
        in_specs=[
            pl.BlockSpec((tm, d), lambda i: (i, 0)),
            _const_spec((1, d)),
            _const_spec((d, dq)),
            _const_spec(sub_keys.shape),
        ],
        out_specs=[
            pl.BlockSpec((tm, width), lambda i: (i, 0)),
            pl.BlockSpec((tm, width), lambda i: (i, 0)),
        ],
        out_shape=[
            jax.ShapeDtypeStruct((t, width), I32),
            jax.ShapeDtypeStruct((t, width), F32),
        ],
        compiler_params=pltpu.CompilerParams(
            dimension_semantics=("arbitrary",), vmem_limit_bytes=VMEM_LIMIT),
        name="retrieve",
    )(h2, g_peer.reshape(1, d), w_query.astype(BF16), sub_keys.astype(BF16))


GROUP = 4
NSETS = 4


def _pack_tables(peer_u, peer_v):
    e, d = peer_u.shape

    def pack(tbl):
        bits = lax.bitcast_convert_type(tbl.astype(BF16), jnp.uint16).astype(jnp.uint32)
        word = bits[:, :d // 2] | (bits[:, d // 2:] << 16)
        return lax.bitcast_convert_type(word, I32).reshape(e, d // 2 // LANES, LANES)

    both = jnp.concatenate([pack(peer_u), pack(peer_v)], axis=1)
    return both.reshape(e, both.shape[1], 1, LANES)


def _unpack_pair(w):
    lo = pltpu.bitcast(w << 16, F32)
    hi = pltpu.bitcast(w & jnp.int32(-65536), F32)
    return lo, hi


def _experts_kernel(h_ref, gpeer_ref, gate_ref, gfin_ref, idx_hbm, tab_hbm, o_ref,
                    idx_smem, xn_ref, idx_sems, *set_refs, tb, width, chunks, final_norm):
    i = pl.program_id(0)
    nblocks = pl.num_programs(0)
    tbw = tb * width
    gw = GROUP * width
    cur = i % 2
    ngroups = tb // GROUP
    sets = tuple(zip(set_refs[:NSETS], set_refs[NSETS:]))

    def idx_copy(block, buf):
        return pltpu.make_async_copy(
            idx_hbm.at[pl.ds(block * tbw, tbw)], idx_smem.at[pl.ds(buf * tbw, tbw)],
            idx_sems.at[buf])

    def row_copy(expert, rset, r, j):
        rows_ref, sems = rset
        return pltpu.make_async_copy(
            tab_hbm.at[expert], rows_ref.at[r, :, pl.ds(j, 1), :], sems.at[r])

    def issue(offset, rset):
        for r in range(GROUP):
            for j in range(width):
                row_copy(idx_smem[offset + r * width + j], rset, r, j).start(priority=j % 2)

    def wait(rset):
        for r in range(GROUP):
            for j in range(width):
                row_copy(0, rset, r, j).wait()

    @pl.when(i == 0)
    def _():
        idx_copy(0, 0).start()
        idx_copy(0, 0).wait()
        for q in range(NSETS - 1):
            issue(q * gw, sets[q])

    nxt_block = jnp.minimum(i + 1, nblocks - 1)
    idx_copy(nxt_block, 1 - cur).start()
    h = h_ref[...]
    xn_ref[...] = _rms(h, gpeer_ref[...])
    idx_copy(nxt_block, 1 - cur).wait()

    eye = (lax.broadcasted_iota(I32, (width, LANES), 0)
           == lax.broadcasted_iota(I32, (width, LANES), 1)).astype(F32)
    lane = lax.broadcasted_iota(I32, (width, 2 * GROUP), 1)
    egroups = width // SUBLANES

    def group(k, q):
        rows_ref = sets[q][0]
        first = q * GROUP
        window = pl.ds(pl.multiple_of(k * (NSETS * GROUP), NSETS * GROUP), NSETS * GROUP)
        wait(sets[q])
        ahead = k * NSETS + q + NSETS - 1
        nxt = jnp.where(ahead < ngroups, cur * tbw + ahead * gw,
                        (1 - cur) * tbw + (ahead - ngroups) * gw)
        issue(nxt, sets[(q + NSETS - 1) % NSETS])

        xg = xn_ref.at[window][first:first + GROUP, :]
        x_hi = xg.astype(BF16).astype(F32)
        x_lo = (xg - x_hi).astype(BF16).astype(F32)
        x_cat = jnp.concatenate([x_hi, x_lo], axis=0)
        a_all = jnp.zeros((width, 2 * GROUP), F32)
        for r in range(GROUP):
            halves = [_unpack_pair(rows_ref[r, s]) for s in range(chunks)]
            lhs = jnp.concatenate([p[0] for p in halves] + [p[1] for p in halves], axis=1)
            a_r = lax.dot_general(lhs, x_cat, NT_DIMS, preferred_element_type=F32)
            a_all = jnp.where((lane == r) | (lane == GROUP + r), a_r, a_all)
        a_col = a_all[:, :GROUP] + a_all[:, GROUP:]
        g_col = lax.dot_general(eye, gate_ref.at[window][first:first + GROUP, :], NT_DIMS,
                                precision=lax.Precision.HIGHEST, preferred_element_type=F32)
        c_all = _gelu(a_col) * g_col

        o_rows = o_ref.at[window]
        for r in range(GROUP):
            c_col = jnp.broadcast_to(c_all[:, r:r + 1], (width, LANES))
            outs = [jnp.zeros((SUBLANES, LANES), F32) for _ in range(2 * chunks)]
            for jg in range(egroups):
                rows = slice(jg * SUBLANES, (jg + 1) * SUBLANES)
                cb = c_col[rows, :]
                for s in range(chunks):
                    lo, hi = _unpack_pair(rows_ref[r, chunks + s, rows, :])
                    outs[s] = outs[s] + cb * lo
                    outs[s + chunks] = outs[s + chunks] + cb * hi
            for c in range(2 * chunks):
                o_rows[first + r:first + r + 1, c * LANES:(c + 1) * LANES] = (
                    jnp.sum(outs[c], axis=0, keepdims=True))

    def group_round(k, carry):
        for q in range(NSETS):
            group(k, q)
        return carry

    lax.fori_loop(0, ngroups // NSETS, group_round, 0)

    @pl.when(i == nblocks - 1)
    def _():
        for q in range(NSETS - 1):
            wait(sets[q])

    y = h + o_ref[...]
    o_ref[...] = _rms(y, gfin_ref[...]) if final_norm else y


def _experts(h2, g_peer, gates, idx, table, g_final, *, tb, final_norm):
    t, d = h2.shape
    width = gates.shape[-1]
    chunks = table.shape[1] // 2
    assert tb % (NSETS * GROUP) == 0, "token groups cycle through the buffer sets per block"
    kern = functools.partial(_experts_kernel, tb=tb, width=width, chunks=chunks,
                             final_norm=final_norm)
    return pl.pallas_call(
        kern,
        grid=(t // tb,),
        in_specs=[
            pl.BlockSpec((tb, d), lambda i: (i, 0)),
            _const_spec((1, d)),
            pl.BlockSpec((tb, width), lambda i: (i, 0)),
            _const_spec((1, d)),
            pl.BlockSpec(memory_space=pl.ANY),
            pl.BlockSpec(memory_space=pl.ANY),
        ],
        out_specs=pl.BlockSpec((tb, d), lambda i: (i, 0)),
        out_shape=jax.ShapeDtypeStruct((t, d), F32),
        scratch_shapes=[
            pltpu.SMEM((2 * tb * width,), I32),
            pltpu.VMEM((tb, d), F32),
            pltpu.SemaphoreType.DMA((2,)),
        ] + [pltpu.VMEM((GROUP, 2 * chunks, width, LANES), I32)] * NSETS
          + [pltpu.SemaphoreType.DMA((GROUP,))] * NSETS,
        compiler_params=pltpu.CompilerParams(
            dimension_semantics=("arbitrary",), vmem_limit_bytes=VMEM_LIMIT),
        name="experts",
    )(h2, g_peer.reshape(1, d), gates, g_final.reshape(1, d), idx.reshape(-1), table)


def kernel(x, mem, g_mix, w_in, sgu_ln_g, sgu_ln_b, sgu_w, sgu_b, pool_w, pool_scale, w_out,
           g_xattn, g_mem, xattn_w_q, xattn_w_kv, xattn_w_o,
           g_peer, peer_w_query, peer_sub_keys, peer_u, peer_v, g_final):
    bn, seq, d = x.shape
    depth = g_mix.shape[0]
    h = x.reshape(bn * seq, d)
    tm = min(256, seq)
    for l in range(depth):
        h = _mixer(h, g_mix[l], w_in[l], sgu_ln_g[l], sgu_ln_b[l], sgu_w[l], sgu_b[l],
                   pool_w[l], pool_scale[l], w_out[l], seq=seq, tm=tm)
        kv = _kv(mem.reshape(-1, d), g_mem[l], xattn_w_kv[l], tn=512)
        k = kv[:, :d].reshape(bn, -1, d)
        v = kv[:, d:].reshape(bn, -1, d)
        h = _xattn(h, g_xattn[l], xattn_w_q[l], k, v, xattn_w_o[l], seq=seq, tm=tm)
        idx, gates = _retrieve(h, g_peer[l], peer_w_query[l], peer_sub_keys[l], tm=tm)
        table = _pack_tables(peer_u[l], peer_v[l])
        h = _experts(h, g_peer[l], gates, idx, table, g_final, tb=min(128, seq),
                     final_norm=(l == depth - 1))
    return h.reshape(bn, seq, d)
```

```python
import functools

import jax
import jax.numpy as jnp
from jax import lax
from jax.experimental import pallas as pl
from jax.experimental.pallas import tpu as pltpu

F32 = jnp.float32
BF16 = jnp.bfloat16
I32 = jnp.int32

EPS = 1e-6
CHUNK = 64
SGU_BLOCK = 128
A_HEADS = 8
POOL_WINDOWS = (2, 4, 8, 16)
POOL_HALO = 16
XATTN_HEADS = 4
PEER_HEADS = 8
PEER_TOPK = 16
LANES = 128
SUBLANES = 8
VMEM_LIMIT = 56 * 1024 * 1024

NT_DIMS = (((1,), (1,)), ((), ()))


def _rms(x, g):
    return x * lax.rsqrt(jnp.mean(x * x, axis=-1, keepdims=True) + EPS) * g


def _gelu(x):
    return 0.5 * x * (1.0 + lax.erf(x * 0.7071067811865476))


def _const_spec(shape):
    nd = len(shape)
    return pl.BlockSpec(shape, lambda i: (0,) * nd)


def _mixer_kernel(x_ref, g_ref, win_ref, lng_ref, lnb_ref, wsgu_ref, bsgu_ref,
                  wpool_ref, pscale_ref, wout_ref, o_ref, xpe_ref, cat_ref,
                  *, tm, blocks_per_seq, d_a, d_b):
    i = pl.program_id(0)
    seq_block = i % blocks_per_seq
    x = x_ref[...]
    nb = _rms(x, g_ref[...]).astype(BF16)

    v = _gelu(jnp.dot(nb, win_ref[:, d_a:2 * d_a], preferred_element_type=F32))
    mu = jnp.mean(v, axis=-1, keepdims=True)
    vc = v - mu
    v = vc * lax.rsqrt(jnp.mean(vc * vc, axis=-1, keepdims=True) + EPS)
    v = (v * lng_ref[...] + lnb_ref[...]).astype(BF16)
    u = _gelu(jnp.dot(nb, win_ref[:, 0:d_a], preferred_element_type=F32))
    hd = d_a // A_HEADS
    pi = lax.broadcasted_iota(I32, (SGU_BLOCK, SGU_BLOCK), 0) // CHUNK
    pj = lax.broadcasted_iota(I32, (SGU_BLOCK, SGU_BLOCK), 1) // CHUNK
    causal = pj <= pi
    for h in range(A_HEADS):
        w = jnp.where(causal, wsgu_ref[h], 0.0).astype(BF16)
        for r in range(tm // SGU_BLOCK):
            rows = slice(r * SGU_BLOCK, (r + 1) * SGU_BLOCK)
            cols = slice(h * hd, (h + 1) * hd)
            mixed = jnp.dot(w, v[rows, cols], preferred_element_type=F32) + bsgu_ref[h]
            cat_ref[rows, cols] = (u[rows, cols] * mixed).astype(BF16)

    xp = jnp.dot(nb, win_ref[:, 2 * d_a:], preferred_element_type=F32)
    tail = xpe_ref[tm:tm + POOL_HALO, :]
    xpe_ref[0:POOL_HALO, :] = jnp.where(seq_block == 0, 0.0, tail)
    xpe_ref[POOL_HALO:, :] = xp
    dg = d_b // len(POOL_WINDOWS)
    t_seq = seq_block * tm + lax.broadcasted_iota(I32, (tm, 1), 0)
    for g, w in enumerate(POOL_WINDOWS):
        cols = slice(g * dg, (g + 1) * dg)
        acc = xp[:, cols]
        for k in range(1, w):
            acc = acc + xpe_ref[POOL_HALO - k:POOL_HALO - k + tm, cols]
        count = jnp.minimum(t_seq + 1, w).astype(F32)
        y = (acc / count - xp[:, cols]).astype(BF16)
        yb = jnp.dot(y, wpool_ref[g], preferred_element_type=F32) * pscale_ref[:, cols]
        cat_ref[:, d_a + g * dg:d_a + (g + 1) * dg] = yb.astype(BF16)

    o_ref[...] = x + jnp.dot(cat_ref[...], wout_ref[...], preferred_element_type=F32)


def _mixer(x2, g_mix, w_in, ln_g, ln_b, sgu_w, sgu_b, pool_w, pool_scale, w_out, *, seq, tm):
    t, d = x2.shape
    d_a = ln_g.shape[-1]
    d_b = pool_scale.shape[-1]
    d_in = w_in.shape[-1]
    bsgu = jnp.broadcast_to(sgu_b[:, :, None], sgu_b.shape + (d_a // A_HEADS,))
    kern = functools.partial(_mixer_kernel, tm=tm, blocks_per_seq=seq // tm, d_a=d_a, d_b=d_b)
    return pl.pallas_call(
        kern,
        grid=(t // tm,),
        in_specs=[
            pl.BlockSpec((tm, d), lambda i: (i, 0)),
            _const_spec((1, d)),
            _const_spec((d, d_in)),
            _const_spec((1, d_a)),
            _const_spec((1, d_a)),
            _const_spec(sgu_w.shape),
            _const_spec(bsgu.shape),
            _const_spec(pool_w.shape),
            _const_spec((1, d_b)),
            _const_spec(w_out.shape),
        ],
        out_specs=pl.BlockSpec((tm, d), lambda i: (i, 0)),
        out_shape=jax.ShapeDtypeStruct((t, d), F32),
        scratch_shapes=[
            pltpu.VMEM((tm + POOL_HALO, d_b), F32),
            pltpu.VMEM((tm, d_a + d_b), BF16),
        ],
        compiler_params=pltpu.CompilerParams(
            dimension_semantics=("arbitrary",), vmem_limit_bytes=VMEM_LIMIT),
        name="mixer",
    )(x2, g_mix.reshape(1, d), w_in.astype(BF16), ln_g.reshape(1, d_a), ln_b.reshape(1, d_a),
      sgu_w, bsgu, pool_w.astype(BF16), pool_scale.reshape(1, d_b), w_out.astype(BF16))


def _kv_kernel(m_ref, g_ref, w_ref, o_ref):
    mb = _rms(m_ref[...], g_ref[...]).astype(BF16)
    o_ref[...] = jnp.dot(mb, w_ref[...], preferred_element_type=F32).astype(BF16)


def _kv(mem2, g_mem, w_kv, *, tn):
    m, d = mem2.shape
    n = w_kv.shape[-1]
    return pl.pallas_call(
        _kv_kernel,
        grid=(n // tn,),
        in_specs=[
            _const_spec((m, d)),
            _const_spec((1, d)),
            pl.BlockSpec((d, tn), lambda j: (0, j)),
        ],
        out_specs=pl.BlockSpec((m, tn), lambda j: (0, j)),
        out_shape=jax.ShapeDtypeStruct((m, n), BF16),
        compiler_params=pltpu.CompilerParams(
            dimension_semantics=("arbitrary",), vmem_limit_bytes=VMEM_LIMIT),
        name="memkv",
    )(mem2, g_mem.reshape(1, d), w_kv.astype(BF16))


def _xattn_kernel(h_ref, g_ref, wq_ref, k_ref, v_ref, wo_ref, o_ref, cat_ref, *, scale):
    h = h_ref[...]
    nb = _rms(h, g_ref[...]).astype(BF16)
    d = h.shape[-1]
    hd = d // XATTN_HEADS
    for a in range(XATTN_HEADS):
        cols = slice(a * hd, (a + 1) * hd)
        q = jnp.dot(nb, wq_ref[:, cols], preferred_element_type=F32).astype(BF16)
        s = lax.dot_general(q, k_ref[0, :, cols], NT_DIMS, preferred_element_type=F32) * scale
        s = s - jnp.max(s, axis=-1, keepdims=True)
        e = jnp.exp(s)
        p = (e / jnp.sum(e, axis=-1, keepdims=True)).astype(BF16)
        cat_ref[:, cols] = jnp.dot(p, v_ref[0, :, cols], preferred_element_type=F32).astype(BF16)
    o_ref[...] = h + jnp.dot(cat_ref[...], wo_ref[...], preferred_element_type=F32)


def _xattn(h2, g_x, w_q, k, v, w_o, *, seq, tm):
    t, d = h2.shape
    mlen = k.shape[1]
    bps = seq // tm
    kern = functools.partial(_xattn_kernel, scale=float(d // XATTN_HEADS) ** -0.5)
    return pl.pallas_call(
        kern,
        grid=(t // tm,),
        in_specs=[
            pl.BlockSpec((tm, d), lambda i: (i, 0)),
            _const_spec((1, d)),
            _const_spec((d, d)),
            pl.BlockSpec((1, mlen, d), lambda i: (i // bps, 0, 0)),
            pl.BlockSpec((1, mlen, d), lambda i: (i // bps, 0, 0)),
            _const_spec((d, d)),
        ],
        out_specs=pl.BlockSpec((tm, d), lambda i: (i, 0)),
        out_shape=jax.ShapeDtypeStruct((t, d), F32),
        scratch_shapes=[pltpu.VMEM((tm, d), BF16)],
        compiler_params=pltpu.CompilerParams(
            dimension_semantics=("arbitrary",), vmem_limit_bytes=VMEM_LIMIT),
        name="xattn",
    )(h2, g_x.reshape(1, d), w_q.astype(BF16), k, v, w_o.astype(BF16))


def _topk_rows(s, k):
    n = s.shape[0]
    row = lax.broadcasted_iota(I32, s.shape, 0)
    rank = jnp.full(s.shape, k, F32)
    vals = []
    for r in range(k):
        m = jnp.max(s, axis=0, keepdims=True)
        ix = jnp.min(jnp.where(s == m, row, n), axis=0, keepdims=True)
        vals.append(m)
        hit = row == ix
        rank = jnp.where(hit, float(r), rank)
        s = jnp.where(hit, -jnp.inf, s)
    return jnp.concatenate(vals, axis=0), rank


_STAIRS = (
    ((0, 0, 8, 0),),
    ((0, 8, 8, 0),),
    ((1, 0, 8, 0),),
    ((2, 0, 5, 0), (4, 0, 3, 5)),
    ((3, 0, 4, 0), (5, 0, 2, 4), (6, 0, 2, 6)),
    ((7, 0, 2, 0),) + tuple((a, 0, 1, a - 6) for a in range(8, 14)),
    ((14, 0, 1, 0), (15, 0, 1, 1)),
)


def _stair_candidates(top_s):
    kk = PEER_TOPK
    lanes = top_s[0].shape[1]
    sub = lax.broadcasted_iota(I32, (SUBLANES, lanes), 0)
    vals, poss = [], []
    for segs in _STAIRS:
        v = jnp.full((SUBLANES, lanes), -jnp.inf, F32)
        ps = jnp.full((SUBLANES, lanes), kk * kk, I32)
        for a, b0, nb, row in segs:
            s2 = top_s[1][b0:b0 + SUBLANES, :]
            if row:
                s2 = pltpu.roll(s2, row, 0)
            here = (sub >= row) & (sub < row + nb)
            v = jnp.where(here, top_s[0][a:a + 1, :] + s2, v)
            ps = jnp.where(here, a * kk + b0 + sub - row, ps)
        vals.append(v)
        poss.append(ps)
    return jnp.concatenate(vals, axis=0), jnp.concatenate(poss, axis=0)


def _retrieve_kernel(h_ref, g_ref, wq_ref, keys_ref, xn_ref, e1_ref, lim_ref, e2_ref, r2_ref):
    nb = _rms(h_ref[...], g_ref[...]).astype(BF16)
    xn_ref[...] = nb
    tm = nb.shape[0]
    half = keys_ref.shape[-1]
    kk = PEER_TOPK
    arow = lax.broadcasted_iota(I32, (kk, tm), 0)
    for hd in range(PEER_HEADS):
        score, top_s, rank = [], [], []
        for p in range(2):
            cols = slice((2 * hd + p) * half, (2 * hd + p + 1) * half)
            q = jnp.dot(nb, wq_ref[:, cols], preferred_element_type=F32).astype(BF16)
            s = lax.dot_general(keys_ref[hd, p], q, NT_DIMS, preferred_element_type=F32)
            ts, rk = _topk_rows(s, kk)
            score.append(s)
            top_s.append(ts)
            rank.append(rk)
        cand_s, pos = _stair_candidates(top_s)
        count = jnp.zeros((kk, tm), F32)
        z = jnp.zeros((1, tm), F32)
        best = top_s[0][0:1, :] + top_s[1][0:1, :]
        for _ in range(kk):
            m = jnp.max(cand_s, axis=0, keepdims=True)
            at = jnp.min(jnp.where(cand_s == m, pos, kk * kk), axis=0, keepdims=True)
            cand_s = jnp.where(pos == at, -jnp.inf, cand_s)
            count = count + (arow == (at // kk)).astype(F32)
            z = z + jnp.exp(m - best)
        lim = jnp.zeros_like(rank[0])
        for a in range(kk):
            lim = jnp.where(rank[0] == float(a), count[a:a + 1, :], lim)
        e1_ref[hd] = jnp.where(rank[0] < kk, jnp.exp(score[0] - top_s[0][0:1, :]), 0.0)
        lim_ref[hd] = lim
        e2 = jnp.where(rank[1] < kk, jnp.exp(score[1] - top_s[1][0:1, :]) / z, 0.0)
        e2_ref[hd] = pltpu.bitcast(e2.astype(BF16), I32)
        r2_ref[hd] = pltpu.bitcast(rank[1].astype(BF16), I32)


def _retrieve(h2, g_peer, w_query, sub_keys, *, tm):
    t, d = h2.shape
    dq = w_query.shape[-1]
    n_keys = sub_keys.shape[2]
    maps = [jax.ShapeDtypeStruct((PEER_HEADS, n_keys, t), F32)] * 2 + [
        jax.ShapeDtypeStruct((PEER_HEADS, n_keys // 2, t), I32)] * 2
    map_spec = pl.BlockSpec((PEER_HEADS, n_keys, tm), lambda i: (0, 0, i))
    pair_spec = pl.BlockSpec((PEER_HEADS, n_keys // 2, tm), lambda i: (0, 0, i))
    return pl.pallas_call(
        _retrieve_kernel,
        grid=(t // tm,),
        in_specs=[
            pl.BlockSpec((tm, d), lambda i: (i, 0)),
            _const_spec((1, d)),
            _const_spec((d, dq)),
            _const_spec(sub_keys.shape),
        ],
        out_specs=[pl.BlockSpec((tm, d), lambda i: (i, 0))] + [map_spec] * 2 + [pair_spec] * 2,
        out_shape=[jax.ShapeDtypeStruct((t, d), BF16)] + maps,
        compiler_params=pltpu.CompilerParams(
            dimension_semantics=("arbitrary",), vmem_limit_bytes=VMEM_LIMIT),
        name="retrieve",
    )(h2, g_peer.reshape(1, d), w_query.astype(BF16), sub_keys.astype(BF16))


EXPERT_TOKENS = 512
EXPERT_CHUNK = 1024


def _experts_kernel(xn_ref, h_ref, gfin_ref, u_ref, vt_ref, e1_ref, lim_ref, e2_ref, r2_ref,
                    o_ref, at_ref, ct_ref, acc_ref, *, n_keys, final_norm):
    j = pl.program_id(1)
    tm = xn_ref.shape[0]

    @pl.when(j == 0)
    def _():
        acc_ref[...] = jnp.zeros_like(acc_ref)

    at_ref[...] = lax.dot_general(u_ref[...], xn_ref[...], NT_DIMS, preferred_element_type=F32)
    for c in range(u_ref.shape[0] // n_keys):
        rows = slice(c * n_keys, (c + 1) * n_keys)
        for lt in range(tm // LANES):
            cols = slice(lt * LANES, (lt + 1) * LANES)
            g = jnp.zeros((n_keys, LANES), BF16)
            for hd in range(PEER_HEADS):
                lim = lim_ref[hd, c:c + 1, cols].astype(BF16)
                e1 = e1_ref[hd, c:c + 1, cols].astype(BF16)
                sel = jnp.where(pltpu.bitcast(r2_ref[hd, :, cols], BF16) < lim,
                                pltpu.bitcast(e2_ref[hd, :, cols], BF16), jnp.zeros((), BF16))
                g = g + sel * e1
            ct_ref[rows, cols] = _gelu(at_ref[rows, cols]).astype(BF16) * g
    acc_ref[...] += jnp.dot(vt_ref[...], ct_ref[...], preferred_element_type=F32)

    @pl.when(j == pl.num_programs(1) - 1)
    def _():
        y = h_ref[...] + acc_ref[...].T
        o_ref[...] = _rms(y, gfin_ref[...]) if final_norm else y


def _experts(xn, h2, maps, u_bf, vt_bf, g_final, *, final_norm):
    t, d = h2.shape
    ne = u_bf.shape[0]
    e1, lim, e2, r2 = maps
    n_keys = e1.shape[1]
    tm, ec = min(EXPERT_TOKENS, t), EXPERT_CHUNK
    keys_per_step = ec // n_keys
    assert keys_per_step % SUBLANES == 0 and ne % ec == 0 and t % tm == 0
    kern = functools.partial(_experts_kernel, n_keys=n_keys, final_norm=final_norm)
    row_spec = pl.BlockSpec((PEER_HEADS, keys_per_step, tm), lambda i, j: (0, j, i))
    map_spec = pl.BlockSpec((PEER_HEADS, n_keys // 2, tm), lambda i, j: (0, 0, i))
    return pl.pallas_call(
        kern,
        grid=(t // tm, ne // ec),
        in_specs=[
            pl.BlockSpec((tm, d), lambda i, j: (i, 0)),
            pl.BlockSpec((tm, d), lambda i, j: (i, 0), pipeline_mode=pl.Buffered(1)),
            pl.BlockSpec((1, d), lambda i, j: (0, 0)),
            pl.BlockSpec((ec, d), lambda i, j: (j, 0)),
            pl.BlockSpec((d, ec), lambda i, j: (0, j)),
            row_spec, row_spec, map_spec, map_spec,
        ],
        out_specs=pl.BlockSpec((tm, d), lambda i, j: (i, 0)),
        out_shape=jax.ShapeDtypeStruct((t, d), F32),
        scratch_shapes=[
            pltpu.VMEM((ec, tm), F32),
            pltpu.VMEM((ec, tm), BF16),
            pltpu.VMEM((d, tm), F32),
        ],
        compiler_params=pltpu.CompilerParams(
            dimension_semantics=("arbitrary", "arbitrary"), vmem_limit_bytes=VMEM_LIMIT),
        name="experts",
    )(xn, h2, g_final.reshape(1, d), u_bf, vt_bf, e1, lim, e2, r2)


def kernel(x, mem, g_mix, w_in, sgu_ln_g, sgu_ln_b, sgu_w, sgu_b, pool_w, pool_scale, w_out,
           g_xattn, g_mem, xattn_w_q, xattn_w_kv, xattn_w_o,
           g_peer, peer_w_query, peer_sub_keys, peer_u, peer_v, g_final):
    bn, seq, d = x.shape
    depth = g_mix.shape[0]
    h = x.reshape(bn * seq, d)
    tm = min(256, seq)
    for l in range(depth):
        h = _mixer(h, g_mix[l], w_in[l], sgu_ln_g[l], sgu_ln_b[l], sgu_w[l], sgu_b[l],
                   pool_w[l], pool_scale[l], w_out[l], seq=seq, tm=tm)
        kv = _kv(mem.reshape(-1, d), g_mem[l], xattn_w_kv[l], tn=512)
        k = kv[:, :d].reshape(bn, -1, d)
        v = kv[:, d:].reshape(bn, -1, d)
        h = _xattn(h, g_xattn[l], xattn_w_q[l], k, v, xattn_w_o[l], seq=seq, tm=tm)
        xn, *maps = _retrieve(h, g_peer[l], peer_w_query[l], peer_sub_keys[l], tm=tm)
        h = _experts(xn, h, maps, peer_u[l].astype(BF16), peer_v[l].astype(BF16).T, g_final,
                     final_norm=(l == depth - 1))
    return h.reshape(bn, seq, d)
```

```python
import functools

import jax
import jax.numpy as jnp
from jax import lax
from jax.experimental import pallas as pl
from jax.experimental.pallas import tpu as pltpu

F32 = jnp.float32
BF16 = jnp.bfloat16
I32 = jnp.int32

EPS = 1e-6
CHUNK = 64
SGU_BLOCK = 128
A_HEADS = 8
POOL_WINDOWS = (2, 4, 8, 16)
POOL_HALO = 16
XATTN_HEADS = 4
PEER_HEADS = 8
PEER_TOPK = 16
LANES = 128
SUBLANES = 8
VMEM_LIMIT = 56 * 1024 * 1024

NT_DIMS = (((1,), (1,)), ((), ()))


def _rms(x, g):
    return x * lax.rsqrt(jnp.mean(x * x, axis=-1, keepdims=True) + EPS) * g


def _gelu(x):
    return 0.5 * x * (1.0 + lax.erf(x * 0.7071067811865476))


def _const_spec(shape):
    nd = len(shape)
    return pl.BlockSpec(shape, lambda i: (0,) * nd)


def _mixer_kernel(x_ref, g_ref, win_ref, lng_ref, lnb_ref, wsgu_ref, bsgu_ref,
                  wpool_ref, pscale_ref, wout_ref, o_ref, xpe_ref, cat_ref,
                  *, tm, blocks_per_seq, d_a, d_b):
    i = pl.program_id(0)
    seq_block = i % blocks_per_seq
    x = x_ref[...]
    nb = _rms(x, g_ref[...]).astype(BF16)

    v = _gelu(jnp.dot(nb, win_ref[:, d_a:2 * d_a], preferred_element_type=F32))
    mu = jnp.mean(v, axis=-1, keepdims=True)
    vc = v - mu
    v = vc * lax.rsqrt(jnp.mean(vc * vc, axis=-1, keepdims=True) + EPS)
    v = (v * lng_ref[...] + lnb_ref[...]).astype(BF16)
    u = _gelu(jnp.dot(nb, win_ref[:, 0:d_a], preferred_element_type=F32))
    hd = d_a // A_HEADS
    pi = lax.broadcasted_iota(I32, (SGU_BLOCK, SGU_BLOCK), 0) // CHUNK
    pj = lax.broadcasted_iota(I32, (SGU_BLOCK, SGU_BLOCK), 1) // CHUNK
    causal = pj <= pi
    for h in range(A_HEADS):
        w = jnp.where(causal, wsgu_ref[h], 0.0).astype(BF16)
        for r in range(tm // SGU_BLOCK):
            rows = slice(r * SGU_BLOCK, (r + 1) * SGU_BLOCK)
            cols = slice(h * hd, (h + 1) * hd)
            mixed = jnp.dot(w, v[rows, cols], preferred_element_type=F32) + bsgu_ref[h]
            cat_ref[rows, cols] = (u[rows, cols] * mixed).astype(BF16)

    xp = jnp.dot(nb, win_ref[:, 2 * d_a:], preferred_element_type=F32)
    tail = xpe_ref[tm:tm + POOL_HALO, :]
    xpe_ref[0:POOL_HALO, :] = jnp.where(seq_block == 0, 0.0, tail)
    xpe_ref[POOL_HALO:, :] = xp
    dg = d_b // len(POOL_WINDOWS)
    t_seq = seq_block * tm + lax.broadcasted_iota(I32, (tm, 1), 0)
    for g, w in enumerate(POOL_WINDOWS):
        cols = slice(g * dg, (g + 1) * dg)
        acc = xp[:, cols]
        for k in range(1, w):
            acc = acc + xpe_ref[POOL_HALO - k:POOL_HALO - k + tm, cols]
        count = jnp.minimum(t_seq + 1, w).astype(F32)
        y = (acc / count - xp[:, cols]).astype(BF16)
        yb = jnp.dot(y, wpool_ref[g], preferred_element_type=F32) * pscale_ref[:, cols]
        cat_ref[:, d_a + g * dg:d_a + (g + 1) * dg] = yb.astype(BF16)

    o_ref[...] = x + jnp.dot(cat_ref[...], wout_ref[...], preferred_element_type=F32)


def _mixer(x2, g_mix, w_in, ln_g, ln_b, sgu_w, sgu_b, pool_w, pool_scale, w_out, *, seq, tm):
    t, d = x2.shape
    d_a = ln_g.shape[-1]
    d_b = pool_scale.shape[-1]
    d_in = w_in.shape[-1]
    bsgu = jnp.broadcast_to(sgu_b[:, :, None], sgu_b.shape + (d_a // A_HEADS,))
    kern = functools.partial(_mixer_kernel, tm=tm, blocks_per_seq=seq // tm, d_a=d_a, d_b=d_b)
    return pl.pallas_call(
        kern,
        grid=(t // tm,),
        in_specs=[
            pl.BlockSpec((tm, d), lambda i: (i, 0)),
            _const_spec((1, d)),
            _const_spec((d, d_in)),
            _const_spec((1, d_a)),
            _const_spec((1, d_a)),
            _const_spec(sgu_w.shape),
            _const_spec(bsgu.shape),
            _const_spec(pool_w.shape),
            _const_spec((1, d_b)),
            _const_spec(w_out.shape),
        ],
        out_specs=pl.BlockSpec((tm, d), lambda i: (i, 0)),
        out_shape=jax.ShapeDtypeStruct((t, d), F32),
        scratch_shapes=[
            pltpu.VMEM((tm + POOL_HALO, d_b), F32),
            pltpu.VMEM((tm, d_a + d_b), BF16),
        ],
        compiler_params=pltpu.CompilerParams(
            dimension_semantics=("arbitrary",), vmem_limit_bytes=VMEM_LIMIT),
        name="mixer",
    )(x2, g_mix.reshape(1, d), w_in.astype(BF16), ln_g.reshape(1, d_a), ln_b.reshape(1, d_a),
      sgu_w, bsgu, pool_w.astype(BF16), pool_scale.reshape(1, d_b), w_out.astype(BF16))


def _kv_kernel(m_ref, g_ref, w_ref, o_ref):
    mb = _rms(m_ref[...], g_ref[...]).astype(BF16)
    o_ref[...] = jnp.dot(mb, w_ref[...], preferred_element_type=F32).astype(BF16)


def _kv(mem2, g_mem, w_kv, *, tn):
    m, d = mem2.shape
    n = w_kv.shape[-1]
    return pl.pallas_call(
        _kv_kernel,
        grid=(n // tn,),
        in_specs=[
            _const_spec((m, d)),
            _const_spec((1, d)),
            pl.BlockSpec((d, tn), lambda j: (0, j)),
        ],
        out_specs=pl.BlockSpec((m, tn), lambda j: (0, j)),
        out_shape=jax.ShapeDtypeStruct((m, n), BF16),
        compiler_params=pltpu.CompilerParams(
            dimension_semantics=("arbitrary",), vmem_limit_bytes=VMEM_LIMIT),
        name="memkv",
    )(mem2, g_mem.reshape(1, d), w_kv.astype(BF16))


def _xattn_kernel(h_ref, g_ref, wq_ref, k_ref, v_ref, wo_ref, o_ref, cat_ref, *, scale):
    h = h_ref[...]
    nb = _rms(h, g_ref[...]).astype(BF16)
    d = h.shape[-1]
    hd = d // XATTN_HEADS
    for a in range(XATTN_HEADS):
        cols = slice(a * hd, (a + 1) * hd)
        q = jnp.dot(nb, wq_ref[:, cols], preferred_element_type=F32).astype(BF16)
        s = lax.dot_general(q, k_ref[0, :, cols], NT_DIMS, preferred_element_type=F32) * scale
        s = s - jnp.max(s, axis=-1, keepdims=True)
        e = jnp.exp(s)
        p = (e / jnp.sum(e, axis=-1, keepdims=True)).astype(BF16)
        cat_ref[:, cols] = jnp.dot(p, v_ref[0, :, cols], preferred_element_type=F32).astype(BF16)
    o_ref[...] = h + jnp.dot(cat_ref[...], wo_ref[...], preferred_element_type=F32)


def _xattn(h2, g_x, w_q, k, v, w_o, *, seq, tm):
    t, d = h2.shape
    mlen = k.shape[1]
    bps = seq // tm
    kern = functools.partial(_xattn_kernel, scale=float(d // XATTN_HEADS) ** -0.5)
    return pl.pallas_call(
        kern,
        grid=(t // tm,),
        in_specs=[
            pl.BlockSpec((tm, d), lambda i: (i, 0)),
            _const_spec((1, d)),
            _const_spec((d, d)),
            pl.BlockSpec((1, mlen, d), lambda i: (i // bps, 0, 0)),
            pl.BlockSpec((1, mlen, d), lambda i: (i // bps, 0, 0)),
            _const_spec((d, d)),
        ],
        out_specs=pl.BlockSpec((tm, d), lambda i: (i, 0)),
        out_shape=jax.ShapeDtypeStruct((t, d), F32),
        scratch_shapes=[pltpu.VMEM((tm, d), BF16)],
        compiler_params=pltpu.CompilerParams(
            dimension_semantics=("arbitrary",), vmem_limit_bytes=VMEM_LIMIT),
        name="xattn",
    )(h2, g_x.reshape(1, d), w_q.astype(BF16), k, v, w_o.astype(BF16))


def _topk_rows(s, k):
    n = s.shape[0]
    row = lax.broadcasted_iota(I32, s.shape, 0).astype(F32)
    rank = jnp.full(s.shape, k, F32)
    vals = []
    for r in range(k):
        m = jnp.max(s, axis=0, keepdims=True)
        ix = jnp.min(jnp.where(s == m, row, float(n)), axis=0, keepdims=True)
        vals.append(m)
        hit = row == ix
        rank = jnp.where(hit, float(r), rank)
        s = jnp.where(hit, -jnp.inf, s)
    return jnp.concatenate(vals, axis=0), rank


_STAIRS = (
    ((0, 0, 8, 0),),
    ((0, 8, 8, 0),),
    ((1, 0, 8, 0),),
    ((2, 0, 5, 0), (4, 0, 3, 5)),
    ((3, 0, 4, 0), (5, 0, 2, 4), (6, 0, 2, 6)),
    ((7, 0, 2, 0),) + tuple((a, 0, 1, a - 6) for a in range(8, 14)),
    ((14, 0, 1, 0), (15, 0, 1, 1)),
)


def _stair_candidates(top_s):
    kk = PEER_TOPK
    lanes = top_s[0].shape[1]
    sub = lax.broadcasted_iota(I32, (SUBLANES, lanes), 0)
    vals, poss = [], []
    for segs in _STAIRS:
        v = jnp.full((SUBLANES, lanes), -jnp.inf, F32)
        ps = jnp.full((SUBLANES, lanes), kk * kk, I32)
        for a, b0, nb, row in segs:
            s2 = top_s[1][b0:b0 + SUBLANES, :]
            if row:
                s2 = pltpu.roll(s2, row, 0)
            here = (sub >= row) & (sub < row + nb)
            v = jnp.where(here, top_s[0][a:a + 1, :] + s2, v)
            ps = jnp.where(here, a * kk + b0 + sub - row, ps)
        vals.append(v)
        poss.append(ps.astype(F32))
    return jnp.concatenate(vals, axis=0), jnp.concatenate(poss, axis=0)


def _retrieve_kernel(h_ref, g_ref, wq_ref, keys_ref, xn_ref, e1_ref, lim_ref, e2_ref, r2_ref,
                     q_ref):
    nb = _rms(h_ref[...], g_ref[...]).astype(BF16)
    xn_ref[...] = nb
    q_ref[...] = jnp.dot(nb, wq_ref[...], preferred_element_type=F32).astype(BF16)
    tm = nb.shape[0]
    half = keys_ref.shape[-1]
    kk = PEER_TOPK
    arow = lax.broadcasted_iota(I32, (kk, tm), 0).astype(F32)
    for hd in range(PEER_HEADS):
        score, top_s, rank = [], [], []
        for p in range(2):
            cols = slice((2 * hd + p) * half, (2 * hd + p + 1) * half)
            s = lax.dot_general(keys_ref[hd, p], q_ref[:, cols], NT_DIMS,
                                preferred_element_type=F32)
            ts, rk = _topk_rows(s, kk)
            score.append(s)
            top_s.append(ts)
            rank.append(rk)
        cand_s, pos = _stair_candidates(top_s)
        count = jnp.zeros((kk, tm), F32)
        z = jnp.zeros((1, tm), F32)
        best = top_s[0][0:1, :] + top_s[1][0:1, :]
        for _ in range(kk):
            m = jnp.max(cand_s, axis=0, keepdims=True)
            at = jnp.min(jnp.where(cand_s == m, pos, float(kk * kk)), axis=0, keepdims=True)
            cand_s = jnp.where(pos == at, -jnp.inf, cand_s)
            count = count + (arow == jnp.floor(at * (1.0 / kk))).astype(F32)
            z = z + jnp.exp(m - best)
        lim = jnp.zeros_like(rank[0])
        for a in range(kk):
            lim = jnp.where(rank[0] == float(a), count[a:a + 1, :], lim)
        e1_ref[hd] = jnp.where(rank[0] < kk, jnp.exp(score[0] - top_s[0][0:1, :]), 0.0)
        lim_ref[hd] = lim
        e2 = jnp.where(rank[1] < kk, jnp.exp(score[1] - top_s[1][0:1, :]) / z, 0.0)
        e2_ref[hd] = pltpu.bitcast(e2.astype(BF16), I32)
        r2_ref[hd] = pltpu.bitcast(rank[1].astype(BF16), I32)


def _retrieve(h2, g_peer, w_query, sub_keys, *, tm):
    t, d = h2.shape
    dq = w_query.shape[-1]
    n_keys = sub_keys.shape[2]
    maps = [jax.ShapeDtypeStruct((PEER_HEADS, n_keys, t), F32)] * 2 + [
        jax.ShapeDtypeStruct((PEER_HEADS, n_keys // 2, t), I32)] * 2
    map_spec = pl.BlockSpec((PEER_HEADS, n_keys, tm), lambda i: (0, 0, i))
    pair_spec = pl.BlockSpec((PEER_HEADS, n_keys // 2, tm), lambda i: (0, 0, i))
    return pl.pallas_call(
        _retrieve_kernel,
        grid=(t // tm,),
        in_specs=[
            pl.BlockSpec((tm, d), lambda i: (i, 0)),
            _const_spec((1, d)),
            _const_spec((d, dq)),
            _const_spec(sub_keys.shape),
        ],
        out_specs=[pl.BlockSpec((tm, d), lambda i: (i, 0))] + [map_spec] * 2 + [pair_spec] * 2,
        out_shape=[jax.ShapeDtypeStruct((t, d), BF16)] + maps,
        scratch_shapes=[pltpu.VMEM((tm, dq), BF16)],
        compiler_params=pltpu.CompilerParams(
            dimension_semantics=("arbitrary",), vmem_limit_bytes=VMEM_LIMIT),
        name="retrieve",
    )(h2, g_peer.reshape(1, d), w_query.astype(BF16), sub_keys.astype(BF16))


EXPERT_TOKENS = 512
EXPERT_CHUNK = 1024


def _experts_kernel(xn_ref, h_ref, gfin_ref, u_ref, vt_ref, e1_ref, lim_ref, e2_ref, r2_ref,
                    o_ref, at_ref, ct_ref, acc_ref, *, n_keys, final_norm):
    j = pl.program_id(1)
    tm = xn_ref.shape[0]

    @pl.when(j == 0)
    def _():
        acc_ref[...] = jnp.zeros_like(acc_ref)

    at_ref[...] = lax.dot_general(u_ref[...], xn_ref[...], NT_DIMS, preferred_element_type=F32)
    for c in range(u_ref.shape[0] // n_keys):
        rows = slice(c * n_keys, (c + 1) * n_keys)
        for lt in range(tm // LANES):
            cols = slice(lt * LANES, (lt + 1) * LANES)
            g = jnp.zeros((n_keys, LANES), BF16)
            for hd in range(PEER_HEADS):
                lim = lim_ref[hd, c:c + 1, cols].astype(BF16)
                e1 = e1_ref[hd, c:c + 1, cols].astype(BF16)
                sel = jnp.where(pltpu.bitcast(r2_ref[hd, :, cols], BF16) < lim,
                                pltpu.bitcast(e2_ref[hd, :, cols], BF16), jnp.zeros((), BF16))
                g = g + sel * e1
            ct_ref[rows, cols] = _gelu(at_ref[rows, cols]).astype(BF16) * g
    acc_ref[...] += jnp.dot(vt_ref[...], ct_ref[...], preferred_element_type=F32)

    @pl.when(j == pl.num_programs(1) - 1)
    def _():
        y = h_ref[...] + acc_ref[...].T
        o_ref[...] = _rms(y, gfin_ref[...]) if final_norm else y


def _experts(xn, h2, maps, u_bf, vt_bf, g_final, *, final_norm):
    t, d = h2.shape
    ne = u_bf.shape[0]
    e1, lim, e2, r2 = maps
    n_keys = e1.shape[1]
    tm, ec = min(EXPERT_TOKENS, t), EXPERT_CHUNK
    keys_per_step = ec // n_keys
    assert keys_per_step % SUBLANES == 0 and ne % ec == 0 and t % tm == 0
    kern = functools.partial(_experts_kernel, n_keys=n_keys, final_norm=final_norm)
    row_spec = pl.BlockSpec((PEER_HEADS, keys_per_step, tm), lambda i, j: (0, j, i))
    map_spec = pl.BlockSpec((PEER_HEADS, n_keys // 2, tm), lambda i, j: (0, 0, i))
    return pl.pallas_call(
        kern,
        grid=(t // tm, ne // ec),
        in_specs=[
            pl.BlockSpec((tm, d), lambda i, j: (i, 0)),
            pl.BlockSpec((tm, d), lambda i, j: (i, 0), pipeline_mode=pl.Buffered(1)),
            pl.BlockSpec((1, d), lambda i, j: (0, 0)),
            pl.BlockSpec((ec, d), lambda i, j: (j, 0)),
            pl.BlockSpec((d, ec), lambda i, j: (0, j)),
            row_spec, row_spec, map_spec, map_spec,
        ],
        out_specs=pl.BlockSpec((tm, d), lambda i, j: (i, 0)),
        out_shape=jax.ShapeDtypeStruct((t, d), F32),
        scratch_shapes=[
            pltpu.VMEM((ec, tm), F32),
            pltpu.VMEM((ec, tm), BF16),
            pltpu.VMEM((d, tm), F32),
        ],
        compiler_params=pltpu.CompilerParams(
            dimension_semantics=("arbitrary", "arbitrary"), vmem_limit_bytes=VMEM_LIMIT),
        name="experts",
    )(xn, h2, g_final.reshape(1, d), u_bf, vt_bf, e1, lim, e2, r2)


def kernel(x, mem, g_mix, w_in, sgu_ln_g, sgu_ln_b, sgu_w, sgu_b, pool_w, pool_scale, w_out,
           g_xattn, g_mem, xattn_w_q, xattn_w_kv, xattn_w_o,
           g_peer, peer_w_query, peer_sub_keys, peer_u, peer_v, g_final):
    bn, seq, d = x.shape
    depth = g_mix.shape[0]
    h = x.reshape(bn * seq, d)
    tm = min(256, seq)
    for l in range(depth):
        h = _mixer(h, g_mix[l], w_in[l], sgu_ln_g[l], sgu_ln_b[l], sgu_w[l], sgu_b[l],
                   pool_w[l], pool_scale[l], w_out[l], seq=seq, tm=tm)
        kv = _kv(mem.reshape(-1, d), g_mem[l], xattn_w_kv[l], tn=512)
        k = kv[:, :d].reshape(bn, -1, d)
        v = kv[:, d:].reshape(bn, -1, d)
        h = _xattn(h, g_xattn[l], xattn_w_q[l], k, v, xattn_w_o[l], seq=seq, tm=tm)
        xn, *maps = _retrieve(h, g_peer[l], peer_w_query[l], peer_sub_keys[l], tm=tm)
        h = _experts(xn, h, maps, peer_u[l].astype(BF16), peer_v[l].astype(BF16).T, g_final,
                     final_norm=(l == depth - 1))
    return h.reshape(bn, seq, d)
```

```python
import functools

import jax
import jax.numpy as jnp
from jax import lax
from jax.experimental import pallas as pl
from jax.experimental.pallas import tpu as pltpu

F32 = jnp.float32
BF16 = jnp.bfloat16
I32 = jnp.int32

EPS = 1e-6
CHUNK = 64
SGU_BLOCK = 128
A_HEADS = 8
POOL_WINDOWS = (2, 4, 8, 16)
POOL_HALO = 16
XATTN_HEADS = 4
XATTN_TOKENS = 512
PEER_HEADS = 8
PEER_TOPK = 16
LANES = 128
SUBLANES = 8
VMEM_LIMIT = 56 * 1024 * 1024

NT_DIMS = (((1,), (1,)), ((), ()))


def _rms(x, g):
    return x * lax.rsqrt(jnp.mean(x * x, axis=-1, keepdims=True) + EPS) * g


def _gelu(x):
    return 0.5 * x * (1.0 + lax.erf(x * 0.7071067811865476))


def _const_spec(shape):
    nd = len(shape)
    return pl.BlockSpec(shape, lambda i: (0,) * nd, pipeline_mode=pl.Buffered(1))


def _mixer_kernel(x_ref, g_ref, win_ref, lng_ref, lnb_ref, wsgu_ref, bsgu_ref,
                  wpool_ref, pscale_ref, wout_ref, o_ref, xpe_ref, cat_ref,
                  *, tm, blocks_per_seq, d_a, d_b):
    i = pl.program_id(0)
    seq_block = i % blocks_per_seq
    x = x_ref[...]
    nb = _rms(x, g_ref[...]).astype(BF16)

    v = _gelu(jnp.dot(nb, win_ref[:, d_a:2 * d_a], preferred_element_type=F32))
    mu = jnp.mean(v, axis=-1, keepdims=True)
    vc = v - mu
    v = vc * lax.rsqrt(jnp.mean(vc * vc, axis=-1, keepdims=True) + EPS)
    v = (v * lng_ref[...] + lnb_ref[...]).astype(BF16)
    u = _gelu(jnp.dot(nb, win_ref[:, 0:d_a], preferred_element_type=F32))
    hd = d_a // A_HEADS
    pi = lax.broadcasted_iota(I32, (SGU_BLOCK, SGU_BLOCK), 0) // CHUNK
    pj = lax.broadcasted_iota(I32, (SGU_BLOCK, SGU_BLOCK), 1) // CHUNK
    causal = pj <= pi
    for h in range(A_HEADS):
        w = jnp.where(causal, wsgu_ref[h], 0.0).astype(BF16)
        for r in range(tm // SGU_BLOCK):
            rows = slice(r * SGU_BLOCK, (r + 1) * SGU_BLOCK)
            cols = slice(h * hd, (h + 1) * hd)
            mixed = jnp.dot(w, v[rows, cols], preferred_element_type=F32) + bsgu_ref[h]
            cat_ref[rows, cols] = (u[rows, cols] * mixed).astype(BF16)

    xp = jnp.dot(nb, win_ref[:, 2 * d_a:], preferred_element_type=F32)
    tail = xpe_ref[tm:tm + POOL_HALO, :]
    xpe_ref[0:POOL_HALO, :] = jnp.where(seq_block == 0, 0.0, tail)
    xpe_ref[POOL_HALO:, :] = xp
    dg = d_b // len(POOL_WINDOWS)
    t_seq = seq_block * tm + lax.broadcasted_iota(I32, (tm, 1), 0)
    for g, w in enumerate(POOL_WINDOWS):
        cols = slice(g * dg, (g + 1) * dg)
        acc = xp[:, cols]
        for k in range(1, w):
            acc = acc + xpe_ref[POOL_HALO - k:POOL_HALO - k + tm, cols]
        count = jnp.minimum(t_seq + 1, w).astype(F32)
        y = (acc / count - xp[:, cols]).astype(BF16)
        yb = jnp.dot(y, wpool_ref[g], preferred_element_type=F32) * pscale_ref[:, cols]
        cat_ref[:, d_a + g * dg:d_a + (g + 1) * dg] = yb.astype(BF16)

    o_ref[...] = x + jnp.dot(cat_ref[...], wout_ref[...], preferred_element_type=F32)


def _mixer(x2, g_mix, w_in, ln_g, ln_b, sgu_w, sgu_b, pool_w, pool_scale, w_out, *, seq, tm):
    t, d = x2.shape
    d_a = ln_g.shape[-1]
    d_b = pool_scale.shape[-1]
    d_in = w_in.shape[-1]
    bsgu = jnp.broadcast_to(sgu_b[:, :, None], sgu_b.shape + (d_a // A_HEADS,))
    kern = functools.partial(_mixer_kernel, tm=tm, blocks_per_seq=seq // tm, d_a=d_a, d_b=d_b)
    return pl.pallas_call(
        kern,
        grid=(t // tm,),
        in_specs=[
            pl.BlockSpec((tm, d), lambda i: (i, 0)),
            _const_spec((1, d)),
            _const_spec((d, d_in)),
            _const_spec((1, d_a)),
            _const_spec((1, d_a)),
            _const_spec(sgu_w.shape),
            _const_spec(bsgu.shape),
            _const_spec(pool_w.shape),
            _const_spec((1, d_b)),
            _const_spec(w_out.shape),
        ],
        out_specs=pl.BlockSpec((tm, d), lambda i: (i, 0)),
        out_shape=jax.ShapeDtypeStruct((t, d), F32),
        scratch_shapes=[
            pltpu.VMEM((tm + POOL_HALO, d_b), F32),
            pltpu.VMEM((tm, d_a + d_b), BF16),
        ],
        compiler_params=pltpu.CompilerParams(
            dimension_semantics=("arbitrary",), vmem_limit_bytes=VMEM_LIMIT),
        name="mixer",
    )(x2, g_mix.reshape(1, d), w_in.astype(BF16), ln_g.reshape(1, d_a), ln_b.reshape(1, d_a),
      sgu_w, bsgu, pool_w.astype(BF16), pool_scale.reshape(1, d_b), w_out.astype(BF16))


def _kv_kernel(m_ref, g_ref, w_ref, o_ref):
    mb = _rms(m_ref[...], g_ref[...]).astype(BF16)
    o_ref[...] = jnp.dot(mb, w_ref[...], preferred_element_type=F32).astype(BF16)


def _kv(mem2, g_mem, w_kv, *, tn):
    m, d = mem2.shape
    n = w_kv.shape[-1]
    return pl.pallas_call(
        _kv_kernel,
        grid=(n // tn,),
        in_specs=[
            _const_spec((m, d)),
            _const_spec((1, d)),
            pl.BlockSpec((d, tn), lambda j: (0, j)),
        ],
        out_specs=pl.BlockSpec((m, tn), lambda j: (0, j)),
        out_shape=jax.ShapeDtypeStruct((m, n), BF16),
        compiler_params=pltpu.CompilerParams(
            dimension_semantics=("arbitrary",), vmem_limit_bytes=VMEM_LIMIT),
        name="memkv",
    )(mem2, g_mem.reshape(1, d), w_kv.astype(BF16))


def _xattn_kernel(h_ref, g_ref, wq_ref, k_ref, v_ref, wo_ref, o_ref, cat_ref, *, scale):
    h = h_ref[...]
    nb = _rms(h, g_ref[...]).astype(BF16)
    d = h.shape[-1]
    hd = d // XATTN_HEADS
    for a in range(XATTN_HEADS):
        cols = slice(a * hd, (a + 1) * hd)
        q = jnp.dot(nb, wq_ref[:, cols], preferred_element_type=F32).astype(BF16)
        s = lax.dot_general(q, k_ref[0, :, cols], NT_DIMS, preferred_element_type=F32) * scale
        s = s - jnp.max(s, axis=-1, keepdims=True)
        e = jnp.exp(s)
        p = (e / jnp.sum(e, axis=-1, keepdims=True)).astype(BF16)
        cat_ref[:, cols] = jnp.dot(p, v_ref[0, :, cols], preferred_element_type=F32).astype(BF16)
    o_ref[...] = h + jnp.dot(cat_ref[...], wo_ref[...], preferred_element_type=F32)


def _xattn(h2, g_x, w_q, k, v, w_o, *, seq, tm):
    t, d = h2.shape
    mlen = k.shape[1]
    bps = seq // tm
    kern = functools.partial(_xattn_kernel, scale=float(d // XATTN_HEADS) ** -0.5)
    return pl.pallas_call(
        kern,
        grid=(t // tm,),
        in_specs=[
            pl.BlockSpec((tm, d), lambda i: (i, 0)),
            _const_spec((1, d)),
            _const_spec((d, d)),
            pl.BlockSpec((1, mlen, d), lambda i: (i // bps, 0, 0)),
            pl.BlockSpec((1, mlen, d), lambda i: (i // bps, 0, 0)),
            _const_spec((d, d)),
        ],
        out_specs=pl.BlockSpec((tm, d), lambda i: (i, 0)),
        out_shape=jax.ShapeDtypeStruct((t, d), F32),
        scratch_shapes=[pltpu.VMEM((tm, d), BF16)],
        compiler_params=pltpu.CompilerParams(
            dimension_semantics=("arbitrary",), vmem_limit_bytes=VMEM_LIMIT),
        name="xattn",
    )(h2, g_x.reshape(1, d), w_q.astype(BF16), k, v, w_o.astype(BF16))


def _topk_rows(s, k):
    n = s.shape[0]
    row = lax.broadcasted_iota(I32, s.shape, 0).astype(F32)
    rank = jnp.full(s.shape, k, F32)
    vals = []
    for r in range(k):
        m = jnp.max(s, axis=0, keepdims=True)
        ix = jnp.min(jnp.where(s == m, row, float(n)), axis=0, keepdims=True)
        vals.append(m)
        hit = row == ix
        rank = jnp.where(hit, float(r), rank)
        s = jnp.where(hit, -jnp.inf, s)
    return jnp.concatenate(vals, axis=0), rank


_STAIRS = (
    ((0, 0, 8, 0),),
    ((0, 8, 8, 0),),
    ((1, 0, 8, 0),),
    ((2, 0, 5, 0), (4, 0, 3, 5)),
    ((3, 0, 4, 0), (5, 0, 2, 4), (6, 0, 2, 6)),
    ((7, 0, 2, 0),) + tuple((a, 0, 1, a - 6) for a in range(8, 14)),
    ((14, 0, 1, 0), (15, 0, 1, 1)),
)


def _stair_candidates(top_s):
    kk = PEER_TOPK
    lanes = top_s[0].shape[1]
    sub = lax.broadcasted_iota(I32, (SUBLANES, lanes), 0)
    vals, poss = [], []
    for segs in _STAIRS:
        v = jnp.full((SUBLANES, lanes), -jnp.inf, F32)
        ps = jnp.full((SUBLANES, lanes), kk * kk, I32)
        for a, b0, nb, row in segs:
            s2 = top_s[1][b0:b0 + SUBLANES, :]
            if row:
                s2 = pltpu.roll(s2, row, 0)
            here = (sub >= row) & (sub < row + nb)
            v = jnp.where(here, top_s[0][a:a + 1, :] + s2, v)
            ps = jnp.where(here, a * kk + b0 + sub - row, ps)
        vals.append(v)
        poss.append(ps.astype(F32))
    return jnp.concatenate(vals, axis=0), jnp.concatenate(poss, axis=0)


def _retrieve_kernel(h_ref, g_ref, wq_ref, keys_ref, xn_ref, e1_ref, lim_ref, e2_ref, r2_ref,
                     q_ref):
    nb = _rms(h_ref[...], g_ref[...]).astype(BF16)
    xn_ref[...] = nb
    q_ref[...] = jnp.dot(nb, wq_ref[...], preferred_element_type=F32).astype(BF16)
    tm = nb.shape[0]
    half = keys_ref.shape[-1]
    kk = PEER_TOPK
    arow = lax.broadcasted_iota(I32, (kk, tm), 0).astype(F32)
    for hd in range(PEER_HEADS):
        score, top_s, rank = [], [], []
        for p in range(2):
            cols = slice((2 * hd + p) * half, (2 * hd + p + 1) * half)
            s = lax.dot_general(keys_ref[hd, p], q_ref[:, cols], NT_DIMS,
                                preferred_element_type=F32)
            ts, rk = _topk_rows(s, kk)
            score.append(s)
            top_s.append(ts)
            rank.append(rk)
        cand_s, pos = _stair_candidates(top_s)
        count = jnp.zeros((kk, tm), F32)
        z = jnp.zeros((1, tm), F32)
        best = top_s[0][0:1, :] + top_s[1][0:1, :]
        for _ in range(kk):
            m = jnp.max(cand_s, axis=0, keepdims=True)
            at = jnp.min(jnp.where(cand_s == m, pos, float(kk * kk)), axis=0, keepdims=True)
            cand_s = jnp.where(pos == at, -jnp.inf, cand_s)
            count = count + (arow == jnp.floor(at * (1.0 / kk))).astype(F32)
            z = z + jnp.exp(m - best)
        lim = jnp.zeros_like(rank[0])
        for a in range(kk):
            lim = jnp.where(rank[0] == float(a), count[a:a + 1, :], lim)
        e1_ref[hd] = jnp.where(rank[0] < kk, jnp.exp(score[0] - top_s[0][0:1, :]), 0.0)
        lim_ref[hd] = lim
        e2 = jnp.where(rank[1] < kk, jnp.exp(score[1] - top_s[1][0:1, :]) / z, 0.0)
        e2_ref[hd] = pltpu.bitcast(e2.astype(BF16), I32)
        r2_ref[hd] = pltpu.bitcast(rank[1].astype(BF16), I32)


def _retrieve(h2, g_peer, w_query, sub_keys, *, tm):
    t, d = h2.shape
    dq = w_query.shape[-1]
    n_keys = sub_keys.shape[2]
    maps = [jax.ShapeDtypeStruct((PEER_HEADS, n_keys, t), F32)] * 2 + [
        jax.ShapeDtypeStruct((PEER_HEADS, n_keys // 2, t), I32)] * 2
    map_spec = pl.BlockSpec((PEER_HEADS, n_keys, tm), lambda i: (0, 0, i))
    pair_spec = pl.BlockSpec((PEER_HEADS, n_keys // 2, tm), lambda i: (0, 0, i))
    return pl.pallas_call(
        _retrieve_kernel,
        grid=(t // tm,),
        in_specs=[
            pl.BlockSpec((tm, d), lambda i: (i, 0)),
            _const_spec((1, d)),
            _const_spec((d, dq)),
            _const_spec(sub_keys.shape),
        ],
        out_specs=[pl.BlockSpec((tm, d), lambda i: (i, 0))] + [map_spec] * 2 + [pair_spec] * 2,
        out_shape=[jax.ShapeDtypeStruct((t, d), BF16)] + maps,
        scratch_shapes=[pltpu.VMEM((tm, dq), BF16)],
        compiler_params=pltpu.CompilerParams(
            dimension_semantics=("arbitrary",), vmem_limit_bytes=VMEM_LIMIT),
        name="retrieve",
    )(h2, g_peer.reshape(1, d), w_query.astype(BF16), sub_keys.astype(BF16))


EXPERT_TOKENS = 512
EXPERT_CHUNK = 1024


def _experts_kernel(xn_ref, h_ref, gfin_ref, u_ref, vt_ref, e1_ref, lim_ref, e2_ref, r2_ref,
                    o_ref, at_ref, ct_ref, acc_ref, *, n_keys, final_norm):
    j = pl.program_id(1)
    tm = xn_ref.shape[0]

    @pl.when(j == 0)
    def _():
        acc_ref[...] = jnp.zeros_like(acc_ref)

    at_ref[...] = lax.dot_general(u_ref[...], xn_ref[...], NT_DIMS, preferred_element_type=F32)
    for c in range(u_ref.shape[0] // n_keys):
        rows = slice(c * n_keys, (c + 1) * n_keys)
        for lt in range(tm // LANES):
            cols = slice(lt * LANES, (lt + 1) * LANES)
            g = jnp.zeros((n_keys, LANES), BF16)
            for hd in range(PEER_HEADS):
                lim = lim_ref[hd, c:c + 1, cols].astype(BF16)
                e1 = e1_ref[hd, c:c + 1, cols].astype(BF16)
                sel = jnp.where(pltpu.bitcast(r2_ref[hd, :, cols], BF16) < lim,
                                pltpu.bitcast(e2_ref[hd, :, cols], BF16), jnp.zeros((), BF16))
                g = g + sel * e1
            ct_ref[rows, cols] = _gelu(at_ref[rows, cols]).astype(BF16) * g
    acc_ref[...] += jnp.dot(vt_ref[...], ct_ref[...], preferred_element_type=F32)

    @pl.when(j == pl.num_programs(1) - 1)
    def _():
        y = h_ref[...] + acc_ref[...].T
        o_ref[...] = _rms(y, gfin_ref[...]) if final_norm else y


def _vt_kernel(v_ref, o_ref):
    o_ref[...] = v_ref[...].T.astype(BF16)


def _transposed_bf16(v, *, rows):
    n, d = v.shape
    return pl.pallas_call(
        _vt_kernel,
        grid=(n // rows,),
        in_specs=[pl.BlockSpec((rows, d), lambda j: (j, 0))],
        out_specs=pl.BlockSpec((d, rows), lambda j: (0, j)),
        out_shape=jax.ShapeDtypeStruct((d, n), BF16),
        compiler_params=pltpu.CompilerParams(
            dimension_semantics=("arbitrary",), vmem_limit_bytes=VMEM_LIMIT),
        name="vt",
    )(v)


def _experts(xn, h2, maps, u_bf, vt_bf, g_final, *, final_norm):
    t, d = h2.shape
    ne = u_bf.shape[0]
    e1, lim, e2, r2 = maps
    n_keys = e1.shape[1]
    tm, ec = min(EXPERT_TOKENS, t), EXPERT_CHUNK
    keys_per_step = ec // n_keys
    assert keys_per_step % SUBLANES == 0 and ne % ec == 0 and t % tm == 0
    kern = functools.partial(_experts_kernel, n_keys=n_keys, final_norm=final_norm)
    row_spec = pl.BlockSpec((PEER_HEADS, keys_per_step, tm), lambda i, j: (0, j, i))
    map_spec = pl.BlockSpec((PEER_HEADS, n_keys // 2, tm), lambda i, j: (0, 0, i))
    return pl.pallas_call(
        kern,
        grid=(t // tm, ne // ec),
        in_specs=[
            pl.BlockSpec((tm, d), lambda i, j: (i, 0)),
            pl.BlockSpec((tm, d), lambda i, j: (i, 0), pipeline_mode=pl.Buffered(1)),
            pl.BlockSpec((1, d), lambda i, j: (0, 0)),
            pl.BlockSpec((ec, d), lambda i, j: (j, 0)),
            pl.BlockSpec((d, ec), lambda i, j: (0, j)),
            row_spec, row_spec, map_spec, map_spec,
        ],
        out_specs=pl.BlockSpec((tm, d), lambda i, j: (i, 0)),
        out_shape=jax.ShapeDtypeStruct((t, d), F32),
        scratch_shapes=[
            pltpu.VMEM((ec, tm), F32),
            pltpu.VMEM((ec, tm), BF16),
            pltpu.VMEM((d, tm), F32),
        ],
        compiler_params=pltpu.CompilerParams(
            dimension_semantics=("arbitrary", "arbitrary"), vmem_limit_bytes=VMEM_LIMIT),
        name="experts",
    )(xn, h2, g_final.reshape(1, d), u_bf, vt_bf, e1, lim, e2, r2)


def kernel(x, mem, g_mix, w_in, sgu_ln_g, sgu_ln_b, sgu_w, sgu_b, pool_w, pool_scale, w_out,
           g_xattn, g_mem, xattn_w_q, xattn_w_kv, xattn_w_o,
           g_peer, peer_w_query, peer_sub_keys, peer_u, peer_v, g_final):
    bn, seq, d = x.shape
    depth = g_mix.shape[0]
    h = x.reshape(bn * seq, d)
    tm = min(256, seq)
    for l in range(depth):
        h = _mixer(h, g_mix[l], w_in[l], sgu_ln_g[l], sgu_ln_b[l], sgu_w[l], sgu_b[l],
                   pool_w[l], pool_scale[l], w_out[l], seq=seq, tm=tm)
        kv = _kv(mem.reshape(-1, d), g_mem[l], xattn_w_kv[l], tn=512)
        k = kv[:, :d].reshape(bn, -1, d)
        v = kv[:, d:].reshape(bn, -1, d)
        h = _xattn(h, g_xattn[l], xattn_w_q[l], k, v, xattn_w_o[l], seq=seq,
                   tm=min(XATTN_TOKENS, seq))
        xn, *maps = _retrieve(h, g_peer[l], peer_w_query[l], peer_sub_keys[l], tm=tm)
        vt = _transposed_bf16(peer_v[l], rows=EXPERT_CHUNK)
        h = _experts(xn, h, maps, peer_u[l].astype(BF16), vt, g_final,
                     final_norm=(l == depth - 1))
    return h.reshape(bn, seq, d)
```
